```python
import math
import jax
import jax.numpy as jnp
from jax import lax
import numpy as np

D_MODEL = 1024
BATCH = 8
SEQ = 2048
DEPTH = 4
DEC_BATCH = 128
DEC_SEQ = 8
PAST_LEN = 16384
PAGE_SIZE = 128

D_MIX = D_MODEL
S5_WIDTH = D_MIX // 2
S5_GROUP = 16
S5_GROUPS = S5_WIDTH // S5_GROUP
S5_STATE = 64
GLA_WIDTH = D_MIX - S5_WIDTH
GLA_HEADS = 4
GLA_DV = GLA_WIDTH // GLA_HEADS
GLA_DK = GLA_DV // 2
GLA_KEY_WIDTH = GLA_HEADS * GLA_DK
GLA_GATE_RANK = 16
GLA_GATE_TAU = 16.0
GLA_CHUNK = 16
N_MEM = 256
XA_HEADS = 4
XA_HEAD_DIM = D_MODEL // XA_HEADS
D_FF = 11 * D_MODEL // 4
N_EXPERTS = 8
TOP_K = 2
N_DENSE = (DEPTH + 1) // 2
N_MOE = DEPTH // 2
EPS = 1e-6
SPLITS = (S5_WIDTH,
          S5_WIDTH + GLA_KEY_WIDTH,
          S5_WIDTH + 2 * GLA_KEY_WIDTH,
          S5_WIDTH + 2 * GLA_KEY_WIDTH + GLA_WIDTH,
          S5_WIDTH + 2 * GLA_KEY_WIDTH + GLA_WIDTH + GLA_GATE_RANK)
IN_COLS = SPLITS[-1] + GLA_WIDTH

kernel_name = 'hymba_s5_gla_memxattn_moe_step'

F32 = jnp.float32


def rmsnorm(x, g):
    xf = x.astype(F32)
    y = xf * lax.rsqrt(jnp.mean(xf * xf, axis=-1, keepdims=True) + EPS)
    return (y * g.astype(F32)).astype(x.dtype)


def s5_discretize(lam_re, lam_im, b_re, b_im, log_dt):
    lam_re = jnp.minimum(lam_re.astype(F32), -1e-4)
    lam_im = lam_im.astype(F32)
    dt = jnp.exp(log_dt.astype(F32))[:, None]
    mag = jnp.exp(lam_re * dt)
    ang = lam_im * dt
    abar_re = mag * jnp.cos(ang)
    abar_im = mag * jnp.sin(ang)
    den = lam_re * lam_re + lam_im * lam_im
    n_re = abar_re - 1.0
    n_im = abar_im
    coef_re = (n_re * lam_re + n_im * lam_im) / den
    coef_im = (n_im * lam_re - n_re * lam_im) / den
    b_re = b_re.astype(F32)
    b_im = b_im.astype(F32)
    bbar_re = coef_re[..., None] * b_re - coef_im[..., None] * b_im
    bbar_im = coef_re[..., None] * b_im + coef_im[..., None] * b_re
    return abar_re, abar_im, bbar_re, bbar_im


def _ssm_combine(e1, e2):
    a1r, a1i, b1r, b1i = e1
    a2r, a2i, b2r, b2i = e2
    return (a2r * a1r - a2i * a1i,
            a2r * a1i + a2i * a1r,
            a2r * b1r - a2i * b1i + b2r,
            a2r * b1i + a2i * b1r + b2i)


def s5_mixer(u, h0_re, h0_im, lam_re, lam_im, b_re, b_im, c_re, c_im, d, log_dt, w_glu, b_glu, g_norm):
    bsz, L, _ = u.shape
    uf = u.astype(F32).reshape(bsz, L, S5_GROUPS, S5_GROUP)
    abar_re, abar_im, bbar_re, bbar_im = s5_discretize(lam_re, lam_im, b_re, b_im, log_dt)
    bu_re = jnp.einsum('blgc,gpc->blgp', uf, bbar_re)
    bu_im = jnp.einsum('blgc,gpc->blgp', uf, bbar_im)
    a_re = jnp.broadcast_to(abar_re, bu_re.shape)
    a_im = jnp.broadcast_to(abar_im, bu_im.shape)
    a_re, a_im, h_re, h_im = lax.associative_scan(_ssm_combine, (a_re, a_im, bu_re, bu_im), axis=1)
    h0r = h0_re.astype(F32)[:, None]
    h0i = h0_im.astype(F32)[:, None]
    h_re, h_im = (h_re + a_re * h0r - a_im * h0i,
                  h_im + a_re * h0i + a_im * h0r)
    y = (jnp.einsum('blgp,gcp->blgc', h_re, c_re.astype(F32))
         - jnp.einsum('blgp,gcp->blgc', h_im, c_im.astype(F32))
         + d.astype(F32) * uf).reshape(bsz, L, S5_WIDTH)
    z = jax.nn.gelu(y)
    z = z * jax.nn.sigmoid(z @ w_glu.astype(F32) + b_glu.astype(F32))
    out = rmsnorm(z, g_norm)
    return out.astype(u.dtype), h_re[:, -1], h_im[:, -1]


def gla_chunked(q, k, v, log_a, s0):
    bsz, L = q.shape[:2]
    c = math.gcd(L, GLA_CHUNK)
    n = L // c
    blk = lambda t: t.reshape(bsz, n, c, *t.shape[2:])
    q, k, v, log_a = blk(q), blk(k), blk(v), blk(log_a)
    bcum = jnp.cumsum(log_a, axis=2)
    causal = jnp.tril(jnp.ones((c, c), dtype=bool))[None, None, :, :, None, None]
    diff = bcum[:, :, :, None] - bcum[:, :, None, :]
    decay = jnp.exp(jnp.where(causal, diff, -jnp.inf))
    attn = jnp.einsum('bntshk,bnshk->bnhts', q[:, :, :, None] * decay, k)
    o_intra = jnp.einsum('bnhts,bnshv->bnthv', attn, v)
    blast = bcum[:, :, -1]
    ds = jnp.einsum('bnshk,bnshv->bnhkv', k * jnp.exp(blast[:, :, None] - bcum), v)

    def step(s, inp):
        dec, d_s = inp
        return dec[..., None] * s + d_s, s

    s_final, s_starts = lax.scan(step, s0, (jnp.moveaxis(jnp.exp(blast), 1, 0), jnp.moveaxis(ds, 1, 0)))
    o_inter = jnp.einsum('bnthk,nbhkv->bnthv', q * jnp.exp(bcum), s_starts)
    o = (o_intra + o_inter).reshape(bsz, L, GLA_HEADS, GLA_DV)
    return o, s_final


def gla_mixer(q, k, v, g_lr, r, s0, w_gate2, b_gate, g_norm):
    bsz, L, _ = q.shape
    qh = q.astype(F32).reshape(bsz, L, GLA_HEADS, GLA_DK) * (GLA_DK ** -0.5)
    kh = k.astype(F32).reshape(bsz, L, GLA_HEADS, GLA_DK)
    vh = v.astype(F32).reshape(bsz, L, GLA_HEADS, GLA_DV)
    log_a = jax.nn.log_sigmoid(g_lr.astype(F32) @ w_gate2.astype(F32) + b_gate.astype(F32)) / GLA_GATE_TAU
    log_a = log_a.reshape(bsz, L, GLA_HEADS, GLA_DK)
    o, s_new = gla_chunked(qh, kh, vh, log_a, s0.astype(F32))
    o = rmsnorm(o, g_norm).reshape(bsz, L, GLA_WIDTH) * jax.nn.silu(r.astype(F32))
    return o.astype(q.dtype), s_new


def hybrid_mixer(h, h0_re, h0_im, s0, P, i):
    proj = h @ P['w_in'][i]
    u, q, k, v, g_lr, r = jnp.split(proj, list(SPLITS), axis=-1)
    y_s5, h_re, h_im = s5_mixer(u, h0_re, h0_im, P['s5_lambda_re'][i], P['s5_lambda_im'][i],
                                P['s5_b_re'][i], P['s5_b_im'][i], P['s5_c_re'][i], P['s5_c_im'][i],
                                P['s5_d'][i], P['s5_log_dt'][i], P['s5_w_glu'][i], P['s5_b_glu'][i],
                                P['s5_norm'][i])
    y_gla, s_new = gla_mixer(q, k, v, g_lr, r, s0, P['gla_w_gate2'][i], P['gla_b_gate'][i], P['gla_norm'][i])
    out = jnp.concatenate([y_s5, y_gla], axis=-1) @ P['w_out'][i]
    return out, h_re, h_im, s_new


def memory_kv(mem, g, wk, wv):
    bsz, m = mem.shape[:2]
    mn = rmsnorm(mem, g)
    k = (mn @ wk).reshape(bsz, m, XA_HEADS, XA_HEAD_DIM)
    v = (mn @ wv).reshape(bsz, m, XA_HEADS, XA_HEAD_DIM)
    return k, v


def cross_attend(h, mem_k, mem_v, wq, wo):
    bsz, L, _ = h.shape
    q = (h @ wq).reshape(bsz, L, XA_HEADS, XA_HEAD_DIM)
    s = jnp.einsum('blhd,bmhd->bhlm', q.astype(F32), mem_k.astype(F32)) * (XA_HEAD_DIM ** -0.5)
    p = jax.nn.softmax(s, axis=-1)
    o = jnp.einsum('bhlm,bmhd->blhd', p, mem_v.astype(F32)).astype(h.dtype)
    return o.reshape(bsz, L, XA_HEADS * XA_HEAD_DIM) @ wo


def swiglu(h, w_gu, w_down):
    g, u = jnp.split(h @ w_gu, 2, axis=-1)
    return (jax.nn.silu(g) * u) @ w_down


def moe_swiglu(h, w_router, w_gu, w_down):
    logits = (h @ w_router).astype(F32)
    top_v, top_i = lax.top_k(logits, TOP_K)
    top_w = jax.nn.softmax(top_v, axis=-1)
    gates = jnp.sum(jax.nn.one_hot(top_i, N_EXPERTS, dtype=F32) * top_w[..., None], axis=-2)
    out = jnp.zeros(h.shape, F32)
    for e in range(N_EXPERTS):
        out = out + gates[..., e:e + 1] * swiglu(h, w_gu[e], w_down[e]).astype(F32)
    return out.astype(h.dtype)


def decoder_layer(x, mem_k, mem_v, h0_re, h0_im, s0, P, i):
    mix, h_re, h_im, s_new = hybrid_mixer(rmsnorm(x, P['norm_mix'][i]), h0_re, h0_im, s0, P, i)
    x = x + mix
    x = x + cross_attend(rmsnorm(x, P['norm_xa'][i]), mem_k, mem_v, P['xa_wq'][i], P['xa_wo'][i])
    h = rmsnorm(x, P['norm_ffn'][i])
    if i % 2 == 0:
        f = swiglu(h, P['ffn_w_gu'][i // 2], P['ffn_w_down'][i // 2])
    else:
        f = moe_swiglu(h, P['moe_router'][i // 2], P['moe_w_gu'][i // 2], P['moe_w_down'][i // 2])
    return x + f, h_re, h_im, s_new


def setup_inputs(seed: int = 0) -> dict:
    key = jax.random.key(seed)
    ks = iter(jax.random.split(key, 48))
    nrm = lambda shape, scale: jax.random.normal(next(ks), shape, F32) * scale
    gain = lambda shape: 1.0 + nrm(shape, 0.02)
    inp = {}
    inp['x_prompt'] = nrm((BATCH, SEQ, D_MODEL), 1.0)
    inp['x_sample'] = nrm((DEC_BATCH, DEC_SEQ, D_MODEL), 1.0)
    inp['mem_prompt'] = nrm((BATCH, N_MEM, D_MODEL), 1.0)
    inp['cache_mem_k'] = nrm((DEPTH, DEC_BATCH, N_MEM, XA_HEADS, XA_HEAD_DIM), 1.0)
    inp['cache_mem_v'] = nrm((DEPTH, DEC_BATCH, N_MEM, XA_HEADS, XA_HEAD_DIM), 1.0)
    inp['state_s5_re'] = nrm((DEPTH, DEC_BATCH, S5_GROUPS, S5_STATE), 0.5)
    inp['state_s5_im'] = nrm((DEPTH, DEC_BATCH, S5_GROUPS, S5_STATE), 0.5)
    inp['state_gla'] = nrm((DEPTH, DEC_BATCH, GLA_HEADS, GLA_DK, GLA_DV), 0.5)
    inp['norm_mix'] = gain((DEPTH, D_MODEL))
    inp['w_in'] = nrm((DEPTH, D_MODEL, IN_COLS), D_MODEL ** -0.5)
    inp['s5_lambda_re'] = -0.5 + nrm((DEPTH, S5_GROUPS, S5_STATE), 0.01)
    inp['s5_lambda_im'] = math.pi * jnp.arange(S5_STATE, dtype=F32) + nrm((DEPTH, S5_GROUPS, S5_STATE), 0.01)
    inp['s5_b_re'] = nrm((DEPTH, S5_GROUPS, S5_STATE, S5_GROUP), (2 * S5_GROUP) ** -0.5)
    inp['s5_b_im'] = nrm((DEPTH, S5_GROUPS, S5_STATE, S5_GROUP), (2 * S5_GROUP) ** -0.5)
    inp['s5_c_re'] = nrm((DEPTH, S5_GROUPS, S5_GROUP, S5_STATE), S5_STATE ** -0.5)
    inp['s5_c_im'] = nrm((DEPTH, S5_GROUPS, S5_GROUP, S5_STATE), S5_STATE ** -0.5)
    inp['s5_d'] = nrm((DEPTH, S5_GROUPS, S5_GROUP), 1.0)
    inp['s5_log_dt'] = jax.random.uniform(next(ks), (DEPTH, S5_GROUPS), F32,
                                          minval=math.log(1e-3), maxval=math.log(1e-1))
    inp['s5_w_glu'] = nrm((DEPTH, S5_WIDTH, S5_WIDTH), S5_WIDTH ** -0.5)
    inp['s5_b_glu'] = nrm((DEPTH, S5_WIDTH), 0.02)
    inp['s5_norm'] = gain((DEPTH, S5_WIDTH))
    inp['gla_w_gate2'] = nrm((DEPTH, GLA_GATE_RANK, GLA_KEY_WIDTH), GLA_GATE_RANK ** -0.5)
    inp['gla_b_gate'] = nrm((DEPTH, GLA_KEY_WIDTH), 0.1)
    inp['gla_norm'] = gain((DEPTH, GLA_DV))
    inp['w_out'] = nrm((DEPTH, D_MIX, D_MODEL), D_MIX ** -0.5)
    inp['norm_xa'] = gain((DEPTH, D_MODEL))
    inp['norm_mem'] = gain((DEPTH, D_MODEL))
    inp['xa_wq'] = nrm((DEPTH, D_MODEL, XA_HEADS * XA_HEAD_DIM), D_MODEL ** -0.5)
    inp['xa_wk'] = nrm((DEPTH, D_MODEL, XA_HEADS * XA_HEAD_DIM), D_MODEL ** -0.5)
    inp['xa_wv'] = nrm((DEPTH, D_MODEL, XA_HEADS * XA_HEAD_DIM), D_MODEL ** -0.5)
    inp['xa_wo'] = nrm((DEPTH, XA_HEADS * XA_HEAD_DIM, D_MODEL), (XA_HEADS * XA_HEAD_DIM) ** -0.5)
    inp['norm_ffn'] = gain((DEPTH, D_MODEL))
    inp['ffn_w_gu'] = nrm((N_DENSE, D_MODEL, 2 * D_FF), D_MODEL ** -0.5)
    inp['ffn_w_down'] = nrm((N_DENSE, D_FF, D_MODEL), D_FF ** -0.5)
    inp['moe_router'] = nrm((N_MOE, D_MODEL, N_EXPERTS), D_MODEL ** -0.5)
    inp['moe_w_gu'] = nrm((N_MOE, N_EXPERTS, D_MODEL, 2 * D_FF), D_MODEL ** -0.5)
    inp['moe_w_down'] = nrm((N_MOE, N_EXPERTS, D_FF, D_MODEL), D_FF ** -0.5)
    inp['norm_final'] = gain((D_MODEL,))
    return inp


def reference(x_prompt, x_sample, mem_prompt, cache_mem_k, cache_mem_v, state_s5_re, state_s5_im, state_gla,
              norm_mix, w_in, s5_lambda_re, s5_lambda_im, s5_b_re, s5_b_im, s5_c_re, s5_c_im, s5_d, s5_log_dt,
              s5_w_glu, s5_b_glu, s5_norm, gla_w_gate2, gla_b_gate, gla_norm, w_out,
              norm_xa, norm_mem, xa_wq, xa_wk, xa_wv, xa_wo,
              norm_ffn, ffn_w_gu, ffn_w_down, moe_router, moe_w_gu, moe_w_down, norm_final):
    P = dict(norm_mix=norm_mix, w_in=w_in, s5_lambda_re=s5_lambda_re, s5_lambda_im=s5_lambda_im,
             s5_b_re=s5_b_re, s5_b_im=s5_b_im, s5_c_re=s5_c_re, s5_c_im=s5_c_im, s5_d=s5_d,
             s5_log_dt=s5_log_dt, s5_w_glu=s5_w_glu, s5_b_glu=s5_b_glu, s5_norm=s5_norm,
             gla_w_gate2=gla_w_gate2, gla_b_gate=gla_b_gate, gla_norm=gla_norm, w_out=w_out,
             norm_xa=norm_xa, xa_wq=xa_wq, xa_wo=xa_wo, norm_ffn=norm_ffn,
             ffn_w_gu=ffn_w_gu, ffn_w_down=ffn_w_down,
             moe_router=moe_router, moe_w_gu=moe_w_gu, moe_w_down=moe_w_down)
    bp = x_prompt.shape[0]
    zero_s5 = jnp.zeros((bp, S5_GROUPS, S5_STATE), F32)
    zero_gla = jnp.zeros((bp, GLA_HEADS, GLA_DK, GLA_DV), F32)
    xp, xs = x_prompt, x_sample
    mk_p, mv_p, s5r_p, s5i_p, gla_p, s5r_s, s5i_s, gla_s = ([] for _ in range(8))
    for i in range(DEPTH):
        mk, mv = memory_kv(mem_prompt, norm_mem[i], xa_wk[i], xa_wv[i])
        xp, hr, hi, sg = decoder_layer(xp, mk, mv, zero_s5, zero_s5, zero_gla, P, i)
        mk_p.append(mk)
        mv_p.append(mv)
        s5r_p.append(hr)
        s5i_p.append(hi)
        gla_p.append(sg)
        xs, hr, hi, sg = decoder_layer(xs, cache_mem_k[i], cache_mem_v[i], state_s5_re[i], state_s5_im[i],
                                       state_gla[i], P, i)
        s5r_s.append(hr)
        s5i_s.append(hi)
        gla_s.append(sg)
    y_prompt = rmsnorm(xp, norm_final)
    y_sample = rmsnorm(xs, norm_final)
    return (y_prompt, y_sample, jnp.stack(mk_p), jnp.stack(mv_p), jnp.stack(s5r_p), jnp.stack(s5i_p),
            jnp.stack(gla_p), jnp.stack(s5r_s), jnp.stack(s5i_s), jnp.stack(gla_s))
```

```python
import functools
import math

import jax
import jax.numpy as jnp
from jax import lax
from jax.experimental import pallas as pl
from jax.experimental.pallas import tpu as pltpu

F32 = jnp.float32
BF16 = jnp.bfloat16
EPS = 1e-6

S5_GROUP = 16
S5_STATE = 64
GLA_HEADS = 4
GLA_DK = 64
GLA_DV = 128
GLA_GATE_TAU = 16.0
XA_HEADS = 4
N_EXPERTS = 8
LANES = 128
SUBLANES = 8
VMEM_LIMIT = 56 * 1024 * 1024


def _cparams(*sem):
    return pltpu.CompilerParams(dimension_semantics=sem, vmem_limit_bytes=VMEM_LIMIT)


def _tile(n, pref):
    t = min(n, pref)
    while n % t:
        t //= 2
    return t


def _rms(x, g):
    return x * lax.rsqrt(jnp.mean(x * x, axis=-1, keepdims=True) + EPS) * g


def _mm(a, b):
    return jnp.dot(a, b, preferred_element_type=F32)


def _mm_nt(a, b):
    return lax.dot_general(a, b, (((1,), (1,)), ((), ())), preferred_element_type=F32)


def _mm_tn(a, b):
    return lax.dot_general(a, b, (((0,), (0,)), ((), ())), preferred_element_type=F32)


def _sigmoid(x):
    return 1.0 / (1.0 + jnp.exp(-x))


def _silu(x):
    return x * _sigmoid(x)


def _full(shape):
    return pl.BlockSpec(shape, lambda *_: (0,) * len(shape))


def _rows(tm, width):
    return pl.BlockSpec((tm, width), lambda i: (i, 0))


def _mix_in_kernel(x_ref, g_ref, wu_ref, wqkvr_ref, wg_ref, wg2_ref, bg_ref, u_ref, qkvr_ref, la_ref):
    h = _rms(x_ref[...], g_ref[...]).astype(BF16)
    u_ref[...] = _mm(h, wu_ref[...])
    qkvr_ref[...] = _mm(h, wqkvr_ref[...])
    glr = _mm(h, wg_ref[...])
    pre = _mm(glr.astype(BF16), wg2_ref[...]) + bg_ref[...]
    logsig = jnp.minimum(pre, 0.0) - jnp.log(1.0 + jnp.exp(-jnp.abs(pre)))
    la_ref[...] = logsig * (1.0 / GLA_GATE_TAU)


def _mix_in(x, g, wu, wqkvr, wg, wg2, bg, *, time_major_batch):
    T, D = x.shape
    su = wu.shape[1]
    if time_major_batch:
        L = T // time_major_batch
        tm = _tile(L, 512)
        nt = L // tm
        u_shape = jax.ShapeDtypeStruct((L, time_major_batch * su), F32)
        u_spec = pl.BlockSpec((tm, su), lambda i: (i % nt, i // nt))
    else:
        tm = _tile(T, 512)
        u_shape = jax.ShapeDtypeStruct((T, su), F32)
        u_spec = _rows(tm, su)
    return pl.pallas_call(
        _mix_in_kernel,
        grid=(T // tm,),
        in_specs=[_rows(tm, D), _full(g.shape), _full(wu.shape), _full(wqkvr.shape),
                  _full(wg.shape), _full(wg2.shape), _full(bg.shape)],
        out_specs=[u_spec, _rows(tm, wqkvr.shape[1]), _rows(tm, wg2.shape[1])],
        out_shape=[u_shape, jax.ShapeDtypeStruct((T, wqkvr.shape[1]), F32),
                   jax.ShapeDtypeStruct((T, wg2.shape[1]), F32)],
        compiler_params=_cparams("arbitrary"),
        name="mix_in",
    )(x, g, wu, wqkvr, wg, wg2, bg)


S5_COLS = 512
S5_ROWS_MM = 256


def _s5_kernel(u_ref, h0r_ref, h0i_ref, ar_ref, ai_ref, bblk_ref, cblk_ref, d_ref, wglu_ref, bglu_ref, gn_ref,
               y_ref, hr_out, hi_out, hbuf, *, tt, bb):
    j = pl.program_id(0)
    ns = hr_out.shape[1]
    rows = tt * bb
    rmm = min(S5_ROWS_MM, rows)

    @pl.when(j == 0)
    def _():
        hr_out[...] = h0r_ref[...]
        hi_out[...] = h0i_ref[...]

    for r in range(rows // rmm):
        rs = slice(r * rmm, (r + 1) * rmm)
        hbuf[rs, :] = _mm(u_ref[rs, :].astype(BF16), bblk_ref[...])

    def row_block(rb, _):
        r0 = pl.multiple_of(rb * SUBLANES, SUBLANES)
        for cb in range(ns // S5_COLS):
            cs = slice(cb * S5_COLS, (cb + 1) * S5_COLS)
            cis = slice(ns + cb * S5_COLS, ns + (cb + 1) * S5_COLS)
            ar = jnp.broadcast_to(ar_ref[:, cs], (SUBLANES, S5_COLS))
            ai = jnp.broadcast_to(ai_ref[:, cs], (SUBLANES, S5_COLS))

            def step(t, carry):
                hr, hi = carry
                rr = pl.multiple_of(t * bb + r0, SUBLANES)
                nr = ar * hr - ai * hi + hbuf[pl.ds(rr, SUBLANES), cs]
                ni = ar * hi + ai * hr + hbuf[pl.ds(rr, SUBLANES), cis]
                hbuf[pl.ds(rr, SUBLANES), cs] = nr
                hbuf[pl.ds(rr, SUBLANES), cis] = ni
                return nr, ni

            hr, hi = lax.fori_loop(0, tt, step, (hr_out[pl.ds(r0, SUBLANES), cs], hi_out[pl.ds(r0, SUBLANES), cs]),
                                   unroll=min(tt, 8))
            hr_out[pl.ds(r0, SUBLANES), cs] = hr
            hi_out[pl.ds(r0, SUBLANES), cs] = hi
        return 0

    lax.fori_loop(0, bb // SUBLANES, row_block, 0)

    for r in range(rows // rmm):
        rs = slice(r * rmm, (r + 1) * rmm)
        u = u_ref[rs, :]
        y = _mm(hbuf[rs, :].astype(BF16), cblk_ref[...]) + d_ref[...] * u
        z = 0.5 * y * (1.0 + jnp.tanh(math.sqrt(2.0 / math.pi) * (y + 0.044715 * (y * y * y))))
        z = z * _sigmoid(_mm(z.astype(BF16), wglu_ref[...]) + bglu_ref[...])
        y_ref[rs, :] = _rms(z, gn_ref[...])


def _s5(u, h0r, h0i, ar, ai, bblk, cblk, d, wglu, bglu, gn, *, L, B):
    W = u.shape[1]
    ns = h0r.shape[1]
    tt = _tile(L, max(8, 512 // B))
    kern = functools.partial(_s5_kernel, tt=tt, bb=B)
    return pl.pallas_call(
        kern,
        grid=(L // tt,),
        in_specs=[_rows(tt * B, W), _full((B, ns)), _full((B, ns)),
                  _full(ar.shape), _full(ai.shape), _full(bblk.shape), _full(cblk.shape), _full(d.shape),
                  _full(wglu.shape), _full(bglu.shape), _full(gn.shape)],
        out_specs=[_rows(tt * B, W), _full((B, ns)), _full((B, ns))],
        out_shape=[jax.ShapeDtypeStruct((L * B, W), F32), jax.ShapeDtypeStruct((B, ns), F32),
                   jax.ShapeDtypeStruct((B, ns), F32)],
        scratch_shapes=[pltpu.VMEM((tt * B, 2 * ns), F32)],
        compiler_params=_cparams("arbitrary"),
        name="s5",
    )(u, h0r, h0i, ar, ai, bblk, cblk, d, wglu, bglu, gn)


def _gla_kernel(qkvr_ref, la_ref, s0_ref, gn_ref, y_ref, s_ref, *, nseq, ls):
    c = pl.program_id(1)
    rows = nseq * ls
    kw = GLA_HEADS * GLA_DK

    @pl.when(c == 0)
    def _():
        s_ref[...] = s0_ref[...]

    ri = lax.broadcasted_iota(jnp.int32, (rows, rows), 0)
    ci = lax.broadcasted_iota(jnp.int32, (rows, rows), 1)
    causal = ri >= ci
    if nseq > 1:
        causal = causal & ((ri // ls) == (ci // ls))
    tril = jnp.where(causal, 1.0, 0.0).astype(BF16)

    la = la_ref[...]
    p0 = la.astype(BF16)
    r1 = la - p0.astype(F32)
    p1 = r1.astype(BF16)
    p2 = (r1 - p1.astype(F32)).astype(BF16)
    bcum = _mm(tril, p0) + _mm(tril, p1) + _mm(tril, p2)

    q = qkvr_ref[:, 0:kw] * (GLA_DK ** -0.5)
    k = qkvr_ref[:, kw:2 * kw]
    qd = (q * jnp.exp(bcum)).astype(BF16)
    kd = (k * jnp.exp(-bcum)).astype(BF16)
    eye = (lax.broadcasted_iota(jnp.int32, (GLA_DK, GLA_DK), 0)
           == lax.broadcasted_iota(jnp.int32, (GLA_DK, GLA_DK), 1))

    for h in range(GLA_HEADS):
        hs = slice(h * GLA_DK, (h + 1) * GLA_DK)
        vs = slice(2 * kw + h * GLA_DV, 2 * kw + (h + 1) * GLA_DV)
        v = qkvr_ref[:, vs].astype(BF16)
        att = jnp.where(causal, _mm_nt(qd[:, hs], kd[:, hs]), 0.0)
        o_intra = _mm(att.astype(BF16), v)
        o_parts = []
        for s in range(nseq):
            rs = slice(s * ls, (s + 1) * ls)
            state = s_ref[s, h]
            o_parts.append(o_intra[rs] + _mm(qd[rs, hs], state.astype(BF16)))
            blast = bcum[(s + 1) * ls - 1:(s + 1) * ls, hs]
            k2 = (k[rs, hs] * jnp.exp(blast - bcum[rs, hs])).astype(BF16)
            dec_col = jnp.sum(jnp.where(eye, jnp.exp(blast), 0.0), axis=1, keepdims=True)
            s_ref[s, h] = dec_col * state + _mm_tn(k2, v[rs])
        o = o_parts[0] if nseq == 1 else jnp.concatenate(o_parts, axis=0)
        r = qkvr_ref[:, 2 * kw + GLA_HEADS * GLA_DV + h * GLA_DV:2 * kw + GLA_HEADS * GLA_DV + (h + 1) * GLA_DV]
        y_ref[:, h * GLA_DV:(h + 1) * GLA_DV] = _rms(o, gn_ref[...]) * _silu(r)


def _gla(qkvr, la, s0, gn, *, B, L):
    T = B * L
    if L >= 64:
        nseq, ls = 1, _tile(L, 64)
    else:
        nseq, ls = _tile(B, 8), L
    rows = nseq * ls
    nc = L // ls
    kern = functools.partial(_gla_kernel, nseq=nseq, ls=ls)
    sblk = (nseq,) + s0.shape[1:]
    return pl.pallas_call(
        kern,
        grid=(B // nseq, nc),
        in_specs=[pl.BlockSpec((rows, qkvr.shape[1]), lambda b, c: (b * nc + c, 0)),
                  pl.BlockSpec((rows, la.shape[1]), lambda b, c: (b * nc + c, 0)),
                  pl.BlockSpec(sblk, lambda b, c: (b, 0, 0, 0)), _full(gn.shape)],
        out_specs=[pl.BlockSpec((rows, GLA_HEADS * GLA_DV), lambda b, c: (b * nc + c, 0)),
                   pl.BlockSpec(sblk, lambda b, c: (b, 0, 0, 0))],
        out_shape=[jax.ShapeDtypeStruct((T, GLA_HEADS * GLA_DV), F32), jax.ShapeDtypeStruct(s0.shape, F32)],
        compiler_params=_cparams("arbitrary", "arbitrary"),
        name="gla",
    )(qkvr, la, s0, gn)


def _mix_out_kernel(x_ref, a_ref, b_ref, wa_ref, wb_ref, o_ref):
    o_ref[...] = (x_ref[...] + _mm(a_ref[...].astype(BF16), wa_ref[...])
                  + _mm(b_ref[...].astype(BF16), wb_ref[...]))


def _mix_out(x, ys5, ygla, wa, wb, *, time_major_batch):
    T, D = x.shape
    wdt = wa.shape[0]
    if time_major_batch:
        L = T // time_major_batch
        tm = _tile(L, 512)
        nt = L // tm
        a_spec = pl.BlockSpec((tm, wdt), lambda i: (i % nt, i // nt))
    else:
        tm = _tile(T, 512)
        a_spec = _rows(tm, wdt)
    return pl.pallas_call(
        _mix_out_kernel,
        grid=(T // tm,),
        in_specs=[_rows(tm, D), a_spec, _rows(tm, wb.shape[0]), _full(wa.shape), _full(wb.shape)],
        out_specs=_rows(tm, D),
        out_shape=jax.ShapeDtypeStruct((T, D), F32),
        compiler_params=_cparams("arbitrary"),
        name="mix_out",
    )(x, ys5, ygla, wa, wb)


def _norm_mm_kernel(x_ref, g_ref, w_ref, o_ref):
    o_ref[...] = _mm(_rms(x_ref[...], g_ref[...]).astype(BF16), w_ref[...])


def _norm_mm(x, g, w):
    T, D = x.shape
    N = w.shape[1]
    tm = _tile(T, 512)
    return pl.pallas_call(
        _norm_mm_kernel,
        grid=(T // tm,),
        in_specs=[_rows(tm, D), _full(g.shape), _full(w.shape)],
        out_specs=_rows(tm, N),
        out_shape=jax.ShapeDtypeStruct((T, N), F32),
        compiler_params=_cparams("arbitrary"),
        name="norm_mm",
    )(x, g, w)


def _mm_res_kernel(x_ref, a_ref, w_ref, o_ref):
    o_ref[...] = x_ref[...] + _mm(a_ref[...].astype(BF16), w_ref[...])


def _mm_res(x, a, w):
    T, D = x.shape
    tm = _tile(T, 512)
    return pl.pallas_call(
        _mm_res_kernel,
        grid=(T // tm,),
        in_specs=[_rows(tm, D), _rows(tm, a.shape[1]), _full(w.shape)],
        out_specs=_rows(tm, D),
        out_shape=jax.ShapeDtypeStruct((T, D), F32),
        compiler_params=_cparams("arbitrary"),
        name="mm_res",
    )(x, a, w)


def _xattn_kernel(q_ref, k_ref, v_ref, o_ref, *, nseq, lq):
    dh = q_ref.shape[1] // XA_HEADS
    scale = dh ** -0.5
    for s in range(nseq):
        rs = slice(s * lq, (s + 1) * lq)
        for h in range(XA_HEADS):
            hs = slice(h * dh, (h + 1) * dh)
            q = q_ref[rs, hs].astype(BF16)
            k = k_ref[s, :, hs].astype(BF16)
            v = v_ref[s, :, hs].astype(BF16)
            sc = _mm_nt(q, k) * scale
            p = jnp.exp(sc - jnp.max(sc, axis=-1, keepdims=True))
            p = p / jnp.sum(p, axis=-1, keepdims=True)
            o_ref[rs, hs] = _mm(p.astype(BF16), v)


def _xattn(q, mk, mv, *, B, L):
    T, D = q.shape
    M = mk.shape[1]
    if L >= 64:
        nseq, lq = 1, _tile(L, 512)
    else:
        nseq, lq = _tile(B, 4), L
    nc = L // lq
    rows = nseq * lq
    kern = functools.partial(_xattn_kernel, nseq=nseq, lq=lq)
    return pl.pallas_call(
        kern,
        grid=(B // nseq, nc),
        in_specs=[pl.BlockSpec((rows, D), lambda b, c: (b * nc + c, 0)),
                  pl.BlockSpec((nseq, M, D), lambda b, c: (b, 0, 0)),
                  pl.BlockSpec((nseq, M, D), lambda b, c: (b, 0, 0))],
        out_specs=pl.BlockSpec((rows, D), lambda b, c: (b * nc + c, 0)),
        out_shape=jax.ShapeDtypeStruct((T, D), F32),
        compiler_params=_cparams("arbitrary", "arbitrary"),
        name="xattn",
    )(q, mk, mv)


FF_CHUNK = 256


def _ffn_kernel(x_ref, g_ref, wg_ref, wu_ref, wd_ref, o_ref, h_scr, acc):
    j = pl.program_id(1)

    @pl.when(j == 0)
    def _():
        h_scr[...] = _rms(x_ref[...], g_ref[...]).astype(BF16)
        acc[...] = jnp.zeros_like(acc)

    h = h_scr[...]
    act = _silu(_mm(h, wg_ref[...])) * _mm(h, wu_ref[...])
    acc[...] += _mm(act.astype(BF16), wd_ref[...])

    @pl.when(j == pl.num_programs(1) - 1)
    def _():
        o_ref[...] = x_ref[...] + acc[...]


def _ffn(x, g, w_gu, w_down):
    T, D = x.shape
    dff = w_down.shape[0]
    tm = _tile(T, 1024)
    nj = dff // FF_CHUNK
    return pl.pallas_call(
        _ffn_kernel,
        grid=(T // tm, nj),
        in_specs=[pl.BlockSpec((tm, D), lambda i, j: (i, 0)), _full(g.shape),
                  pl.BlockSpec((D, FF_CHUNK), lambda i, j: (0, j)),
                  pl.BlockSpec((D, FF_CHUNK), lambda i, j: (0, nj + j)),
                  pl.BlockSpec((FF_CHUNK, D), lambda i, j: (j, 0))],
        out_specs=pl.BlockSpec((tm, D), lambda i, j: (i, 0)),
        out_shape=jax.ShapeDtypeStruct((T, D), F32),
        scratch_shapes=[pltpu.VMEM((tm, D), BF16), pltpu.VMEM((tm, D), F32)],
        compiler_params=_cparams("arbitrary", "arbitrary"),
        name="ffn",
    )(x, g, w_gu, w_gu, w_down)


def _router_kernel(x_ref, g_ref, wr_ref, h_ref, idx_ref, wt_ref):
    h = _rms(x_ref[...], g_ref[...])
    h_ref[...] = h.astype(BF16)
    logits = jnp.dot(h, wr_ref[...], preferred_element_type=F32, precision=lax.Precision.HIGHEST)
    lane = lax.broadcasted_iota(jnp.int32, logits.shape, 1)
    neg = jnp.float32(-jnp.inf)
    logits = jnp.where(lane < N_EXPERTS, logits, neg)
    v1 = jnp.max(logits, axis=-1, keepdims=True)
    i1 = jnp.min(jnp.where(logits == v1, lane, LANES), axis=-1, keepdims=True)
    rest = jnp.where(lane == i1, neg, logits)
    v2 = jnp.max(rest, axis=-1, keepdims=True)
    i2 = jnp.min(jnp.where(rest == v2, lane, LANES), axis=-1, keepdims=True)
    e2 = jnp.exp(v2 - v1)
    w1 = 1.0 / (1.0 + e2)
    w2 = e2 / (1.0 + e2)
    idx_ref[...] = jnp.where(lane == 0, i1, jnp.where(lane == 1, i2, 0))
    wt_ref[...] = jnp.where(lane == 0, w1, jnp.where(lane == 1, w2, 0.0))


def _router(x, g, wr):
    T, D = x.shape
    tm = _tile(T, 512)
    return pl.pallas_call(
        _router_kernel,
        grid=(T // tm,),
        in_specs=[_rows(tm, D), _full(g.shape), _full(wr.shape)],
        out_specs=[_rows(tm, D), _rows(tm, LANES), _rows(tm, LANES)],
        out_shape=[jax.ShapeDtypeStruct((T, D), BF16), jax.ShapeDtypeStruct((T, LANES), jnp.int32),
                   jax.ShapeDtypeStruct((T, LANES), F32)],
        compiler_params=_cparams("arbitrary"),
        name="router",
    )(x, g, wr)


def _moe_kernel(te_ref, nu_ref, h_ref, gate_ref, wg_ref, wu_ref, wd_ref, o_ref, acc):
    i = pl.program_id(0)
    j = pl.program_id(1)
    used = i < nu_ref[0]

    @pl.when(j == 0)
    def _():
        acc[...] = jnp.zeros_like(acc)

    @pl.when(used)
    def _():
        h = h_ref[...]
        act = _silu(_mm(h, wg_ref[0])) * _mm(h, wu_ref[0])
        acc[...] += _mm(act.astype(BF16), wd_ref[0])

    @pl.when(j == pl.num_programs(1) - 1)
    def _():
        o_ref[...] = gate_ref[...] * acc[...]


def _moe_ffn(hs, gate, tile_expert, n_used, w_gu, w_down, *, tm):
    Tp, D = hs.shape
    dff = w_down.shape[1]
    nj = dff // FF_CHUNK
    grid_spec = pltpu.PrefetchScalarGridSpec(
        num_scalar_prefetch=2,
        grid=(Tp // tm, nj),
        in_specs=[pl.BlockSpec((tm, D), lambda i, j, te, nu: (i, 0)),
                  pl.BlockSpec((tm, 1), lambda i, j, te, nu: (i, 0)),
                  pl.BlockSpec((1, D, FF_CHUNK), lambda i, j, te, nu: (te[i], 0, j)),
                  pl.BlockSpec((1, D, FF_CHUNK), lambda i, j, te, nu: (te[i], 0, nj + j)),
                  pl.BlockSpec((1, FF_CHUNK, D), lambda i, j, te, nu: (te[i], j, 0))],
        out_specs=pl.BlockSpec((tm, D), lambda i, j, te, nu: (i, 0)),
        scratch_shapes=[pltpu.VMEM((tm, D), F32)],
    )
    return pl.pallas_call(
        _moe_kernel,
        grid_spec=grid_spec,
        out_shape=jax.ShapeDtypeStruct((Tp, D), F32),
        compiler_params=_cparams("arbitrary", "arbitrary"),
        name="moe_ffn",
    )(tile_expert, n_used, hs, gate, w_gu, w_gu, w_down)


def _moe(xs, g, wr, w_gu, w_down):
    sizes = [x.shape[0] for x in xs]
    T = sum(sizes)
    D = xs[0].shape[1]
    tm = _tile(T, 512)
    routed = [_router(x, g, wr) for x in xs]
    h = jnp.concatenate([r[0] for r in routed], axis=0)
    idx = jnp.concatenate([r[1][:, :2] for r in routed], axis=0)
    wts = jnp.concatenate([r[2][:, :2] for r in routed], axis=0)

    flat_e = idx.reshape(-1)
    onehot = (flat_e[:, None] == jnp.arange(N_EXPERTS, dtype=jnp.int32)[None, :]).astype(jnp.int32)
    csum = jnp.cumsum(onehot, axis=0)
    rank = jnp.sum(jnp.where(onehot > 0, csum, 0), axis=1) - 1
    counts = csum[-1]
    padded = ((counts + tm - 1) // tm) * tm
    ends = jnp.cumsum(padded)
    starts = ends - padded
    dest = jnp.sum(jnp.where(onehot > 0, starts[None, :], 0), axis=1) + rank
    n_tiles = (2 * T) // tm + N_EXPERTS
    Tp = n_tiles * tm
    tile_start = jnp.arange(n_tiles, dtype=jnp.int32) * tm
    tile_expert = jnp.minimum(jnp.sum((tile_start[:, None] >= ends[None, :]).astype(jnp.int32), axis=1),
                              N_EXPERTS - 1).astype(jnp.int32)
    n_used = (ends[-1] // tm).astype(jnp.int32).reshape(1)
    src = jnp.zeros((Tp,), jnp.int32).at[dest].set(jnp.arange(2 * T, dtype=jnp.int32) // 2)
    gate = jnp.zeros((Tp,), F32).at[dest].set(wts.reshape(-1))
    hs = jnp.take(h, src, axis=0)
    y = _moe_ffn(hs, gate.reshape(Tp, 1), tile_expert, n_used, w_gu, w_down, tm=tm)
    d2 = dest.reshape(T, 2)
    f = jnp.take(y, d2[:, 0], axis=0) + jnp.take(y, d2[:, 1], axis=0)
    outs, off = [], 0
    for x, n in zip(xs, sizes):
        outs.append(x + f[off:off + n])
        off += n
    return outs


def _norm_kernel(x_ref, g_ref, o_ref):
    o_ref[...] = _rms(x_ref[...], g_ref[...])


def _norm(x, g):
    T, D = x.shape
    tm = _tile(T, 1024)
    return pl.pallas_call(
        _norm_kernel,
        grid=(T // tm,),
        in_specs=[_rows(tm, D), _full(g.shape)],
        out_specs=_rows(tm, D),
        out_shape=jax.ShapeDtypeStruct((T, D), F32),
        compiler_params=_cparams("arbitrary"),
        name="final_norm",
    )(x, g)


def _s5_params(lam_re, lam_im, b_re, b_im, c_re, c_im, d, log_dt):
    G, P = lam_re.shape
    lam_re = jnp.minimum(lam_re, -1e-4)
    dt = jnp.exp(log_dt)[:, None]
    mag = jnp.exp(lam_re * dt)
    ang = lam_im * dt
    abar_re = mag * jnp.cos(ang)
    abar_im = mag * jnp.sin(ang)
    den = lam_re * lam_re + lam_im * lam_im
    n_re = abar_re - 1.0
    n_im = abar_im
    coef_re = (n_re * lam_re + n_im * lam_im) / den
    coef_im = (n_im * lam_re - n_re * lam_im) / den
    bbar_re = coef_re[..., None] * b_re - coef_im[..., None] * b_im
    bbar_im = coef_re[..., None] * b_im + coef_im[..., None] * b_re
    eye = jnp.eye(G, dtype=F32)
    blk = lambda t: jnp.einsum('gpc,gh->gchp', t, eye).reshape(G * S5_GROUP, G * P)
    bblk = jnp.concatenate([blk(bbar_re), blk(bbar_im)], axis=1).astype(BF16)
    cb = lambda t: jnp.einsum('gcp,gh->gphc', t, eye).reshape(G * P, G * S5_GROUP)
    cblk = jnp.concatenate([cb(c_re), cb(-c_im)], axis=0).astype(BF16)
    return (abar_re.reshape(1, G * P), abar_im.reshape(1, G * P), bblk, cblk, d.reshape(1, G * S5_GROUP))


def kernel(x_prompt, x_sample, mem_prompt, cache_mem_k, cache_mem_v, state_s5_re, state_s5_im, state_gla,
           norm_mix, w_in, s5_lambda_re, s5_lambda_im, s5_b_re, s5_b_im, s5_c_re, s5_c_im, s5_d, s5_log_dt,
           s5_w_glu, s5_b_glu, s5_norm, gla_w_gate2, gla_b_gate, gla_norm, w_out,
           norm_xa, norm_mem, xa_wq, xa_wk, xa_wv, xa_wo,
           norm_ffn, ffn_w_gu, ffn_w_down, moe_router, moe_w_gu, moe_w_down, norm_final):
    Bp, Lp, D = x_prompt.shape
    Bs, Ls, _ = x_sample.shape
    depth = w_in.shape[0]
    M = mem_prompt.shape[1]
    G, P = s5_lambda_re.shape[1:]
    ns = G * P
    s5w = G * S5_GROUP
    kw = GLA_HEADS * GLA_DK
    vw = GLA_HEADS * GLA_DV
    rank = gla_w_gate2.shape[1]

    xp = x_prompt.reshape(Bp * Lp, D)
    xs = x_sample.reshape(Bs * Ls, D)
    mem = mem_prompt.reshape(Bp * M, D)
    row = lambda v: v.reshape(1, -1)
    zeros_s5 = jnp.zeros((Bp, ns), F32)
    zeros_gla = jnp.zeros((Bp, GLA_HEADS, GLA_DK, GLA_DV), F32)

    outs = {k: [] for k in ('mk', 'mv', 'pr', 'pi', 'pg', 'sr', 'si', 'sg')}
    for i in range(depth):
        w = w_in[i]
        wu = w[:, :s5w].astype(BF16)
        wqkvr = jnp.concatenate([w[:, s5w:s5w + 2 * kw + vw], w[:, s5w + 2 * kw + vw + rank:]], axis=1).astype(BF16)
        wg = jnp.pad(w[:, s5w + 2 * kw + vw:s5w + 2 * kw + vw + rank], ((0, 0), (0, LANES - rank))).astype(BF16)
        wg2 = jnp.pad(gla_w_gate2[i], ((0, LANES - rank), (0, 0))).astype(BF16)
        ar, ai, bblk, cblk, dvec = _s5_params(s5_lambda_re[i], s5_lambda_im[i], s5_b_re[i], s5_b_im[i],
                                              s5_c_re[i], s5_c_im[i], s5_d[i], s5_log_dt[i])
        wglu = s5_w_glu[i].astype(BF16)
        wo_a = w_out[i][:s5w].astype(BF16)
        wo_b = w_out[i][s5w:].astype(BF16)
        wq = xa_wq[i].astype(BF16)
        wkv = jnp.concatenate([xa_wk[i], xa_wv[i]], axis=1).astype(BF16)
        wo = xa_wo[i].astype(BF16)

        kv = _norm_mm(mem, row(norm_mem[i]), wkv)
        mk = kv[:, :D].reshape(Bp, M, D)
        mv = kv[:, D:].reshape(Bp, M, D)
        outs['mk'].append(mk.reshape(Bp, M, XA_HEADS, D // XA_HEADS))
        outs['mv'].append(mv.reshape(Bp, M, XA_HEADS, D // XA_HEADS))

        new_x = []
        for grp, x, B, L in (('p', xp, Bp, Lp), ('s', xs, Bs, Ls)):
            prompt = grp == 'p'
            u, qkvr, la = _mix_in(x, row(norm_mix[i]), wu, wqkvr, wg, wg2, row(gla_b_gate[i]),
                                  time_major_batch=B if prompt else 0)
            if prompt:
                u_tm = u.reshape(L * B, s5w)
                h0r, h0i, s0 = zeros_s5, zeros_s5, zeros_gla
            else:
                u_tm = u.reshape(B, L, s5w).transpose(1, 0, 2).reshape(L * B, s5w)
                h0r = state_s5_re[i].reshape(B, ns)
                h0i = state_s5_im[i].reshape(B, ns)
                s0 = state_gla[i]
            y5, hr, hi = _s5(u_tm, h0r, h0i, ar, ai, bblk, cblk, dvec, wglu, row(s5_b_glu[i]), row(s5_norm[i]),
                             L=L, B=B)
            if prompt:
                y5 = y5.reshape(L, B * s5w)
            else:
                y5 = y5.reshape(L, B, s5w).transpose(1, 0, 2).reshape(B * L, s5w)
            yg, sg = _gla(qkvr, la, s0, row(gla_norm[i]), B=B, L=L)
            outs[grp + 'r'].append(hr.reshape(B, G, P))
            outs[grp + 'i'].append(hi.reshape(B, G, P))
            outs[grp + 'g'].append(sg)
            x = _mix_out(x, y5, yg, wo_a, wo_b, time_major_batch=B if prompt else 0)
            q = _norm_mm(x, row(norm_xa[i]), wq)
            if prompt:
                o = _xattn(q, mk, mv, B=B, L=L)
            else:
                o = _xattn(q, cache_mem_k[i].reshape(B, M, D), cache_mem_v[i].reshape(B, M, D), B=B, L=L)
            x = _mm_res(x, o, wo)
            new_x.append(x)
        xp, xs = new_x
        if i % 2 == 0:
            wgu = ffn_w_gu[i // 2].astype(BF16)
            wd = ffn_w_down[i // 2].astype(BF16)
            xp = _ffn(xp, row(norm_ffn[i]), wgu, wd)
            xs = _ffn(xs, row(norm_ffn[i]), wgu, wd)
        else:
            wr = jnp.pad(moe_router[i // 2], ((0, 0), (0, LANES - N_EXPERTS)))
            xp, xs = _moe([xp, xs], row(norm_ffn[i]), wr, moe_w_gu[i // 2].astype(BF16),
                          moe_w_down[i // 2].astype(BF16))

    y_prompt = _norm(xp, row(norm_final)).reshape(Bp, Lp, D)
    y_sample = _norm(xs, row(norm_final)).reshape(Bs, Ls, D)
    st = jnp.stack
    return (y_prompt, y_sample, st(outs['mk']), st(outs['mv']), st(outs['pr']), st(outs['pi']), st(outs['pg']),
            st(outs['sr']), st(outs['si']), st(outs['sg']))
```

```python
import functools
import math

import jax
import jax.numpy as jnp
from jax import lax
from jax.experimental import pallas as pl
from jax.experimental.pallas import tpu as pltpu

F32 = jnp.float32
BF16 = jnp.bfloat16
EPS = 1e-6

S5_GROUP = 16
GLA_HEADS = 4
GLA_DK = 64
GLA_DV = 128
GLA_GATE_TAU = 16.0
XA_HEADS = 4
N_EXPERTS = 8
LANES = 128
SUBLANES = 8
VMEM_LIMIT = 56 * 1024 * 1024


def _cparams(*sem):
    return pltpu.CompilerParams(dimension_semantics=sem, vmem_limit_bytes=VMEM_LIMIT)


def _tile(n, pref):
    t = min(n, pref)
    while n % t:
        t //= 2
    return t


def _rms(x, g):
    return x * lax.rsqrt(jnp.mean(x * x, axis=-1, keepdims=True) + EPS) * g


def _mm(a, b):
    return jnp.dot(a, b, preferred_element_type=F32)


def _mm_nt(a, b):
    return lax.dot_general(a, b, (((1,), (1,)), ((), ())), preferred_element_type=F32)


def _mm_tn(a, b):
    return lax.dot_general(a, b, (((0,), (0,)), ((), ())), preferred_element_type=F32)


def _sigmoid(x):
    return 1.0 / (1.0 + jnp.exp(-x))


def _silu(x):
    return x * _sigmoid(x)


def _full(shape):
    return pl.BlockSpec(shape, lambda *_: (0,) * len(shape))


def _rows(tm, width):
    return pl.BlockSpec((tm, width), lambda i, *_: (i, 0))


def _layer(arr, layer):
    nd = arr.ndim - 1
    return pl.BlockSpec((None,) + arr.shape[1:], lambda *_: (layer,) + (0,) * nd, pipeline_mode=pl.Buffered(1))


def _first_step(*axes):
    cond = pl.program_id(axes[0]) == 0
    for a in axes[1:]:
        cond = cond & (pl.program_id(a) == 0)
    return cond


def _mix_in_kernel(x_ref, g_ref, w_ref, wg2_ref, bg_ref, u_ref, qkv_ref, r_ref, la_ref,
                   wu_s, wqkv_s, wgl_s, wr_s, wg2_s, *, su, sqkv, rank):
    @pl.when(_first_step(0))
    def _():
        wu_s[...] = w_ref[:, 0:su].astype(BF16)
        wqkv_s[...] = w_ref[:, su:su + sqkv].astype(BF16)
        wgl_s[...] = w_ref[:, su + sqkv:su + sqkv + LANES].astype(BF16)
        wr_s[...] = w_ref[:, su + sqkv + rank:].astype(BF16)
        wg2_s[...] = jnp.zeros_like(wg2_s)
        wg2_s[0:rank, :] = wg2_ref[...].astype(BF16)

    h = _rms(x_ref[...], g_ref[...]).astype(BF16)
    u_ref[...] = _mm(h, wu_s[...])
    qkv_ref[...] = _mm(h, wqkv_s[...])
    r_ref[...] = _mm(h, wr_s[...])
    glr = _mm(h, wgl_s[...])
    pre = _mm(glr.astype(BF16), wg2_s[...]) + bg_ref[...]
    logsig = jnp.minimum(pre, 0.0) - jnp.log(1.0 + jnp.exp(-jnp.abs(pre)))
    la_ref[...] = logsig * (1.0 / GLA_GATE_TAU)


def _mix_in(x, g, w_in, wg2, bg, layer, *, su, sqkv, sr, time_major_batch):
    T, D = x.shape
    rank = wg2.shape[1]
    kw = wg2.shape[2]
    if time_major_batch:
        L = T // time_major_batch
        tm = _tile(L, 512)
        nt = L // tm
        u_shape = jax.ShapeDtypeStruct((L, time_major_batch * su), F32)
        u_spec = pl.BlockSpec((tm, su), lambda i: (i % nt, i // nt))
    else:
        tm = _tile(T, 512)
        u_shape = jax.ShapeDtypeStruct((T, su), F32)
        u_spec = _rows(tm, su)
    kern = functools.partial(_mix_in_kernel, su=su, sqkv=sqkv, rank=rank)
    return pl.pallas_call(
        kern,
        grid=(T // tm,),
        in_specs=[_rows(tm, D), _layer(g, layer), _layer(w_in, layer), _layer(wg2, layer), _layer(bg, layer)],
        out_specs=[u_spec, _rows(tm, sqkv), _rows(tm, sr), _rows(tm, kw)],
        out_shape=[u_shape, jax.ShapeDtypeStruct((T, sqkv), F32), jax.ShapeDtypeStruct((T, sr), F32),
                   jax.ShapeDtypeStruct((T, kw), F32)],
        scratch_shapes=[pltpu.VMEM((D, su), BF16), pltpu.VMEM((D, sqkv), BF16), pltpu.VMEM((D, LANES), BF16),
                        pltpu.VMEM((D, sr), BF16), pltpu.VMEM((LANES, kw), BF16)],
        compiler_params=_cparams("arbitrary"),
        name="mix_in",
    )(x, g, w_in, wg2, bg)


def _s5_disc_kernel(lr_ref, li_ref, ldt_ref, ar_ref, ai_ref, cr_ref, ci_ref):
    lam_re = jnp.minimum(lr_ref[...], -1e-4)
    lam_im = li_ref[...]
    dt = jnp.exp(ldt_ref[...])
    mag = jnp.exp(lam_re * dt)
    ang = lam_im * dt
    abar_re = mag * jnp.cos(ang)
    abar_im = mag * jnp.sin(ang)
    den = lam_re * lam_re + lam_im * lam_im
    n_re = abar_re - 1.0
    n_im = abar_im
    ar_ref[...] = abar_re
    ai_ref[...] = abar_im
    cr_ref[...] = (n_re * lam_re + n_im * lam_im) / den
    ci_ref[...] = (n_im * lam_re - n_re * lam_im) / den


def _s5_disc(lam_re, lam_im, log_dt):
    shp = lam_re.shape
    out = jax.ShapeDtypeStruct(shp, F32)
    return pl.pallas_call(
        _s5_disc_kernel,
        in_specs=[_full(shp), _full(shp), _full(log_dt.shape)],
        out_specs=[_full(shp)] * 4,
        out_shape=[out] * 4,
        name="s5_disc",
    )(lam_re, lam_im, log_dt)


S5_CH = 128
S5_ROWS_MM = 256


def _s5_kernel(u_ref, h0r_ref, h0i_ref, ar_ref, ai_ref, bq_ref, cq_ref, d_ref, wglu_ref, bglu_ref, gn_ref,
               y_ref, hr_out, hi_out, hbuf, ybuf, wglu_s, *, tt, bb):
    j = pl.program_id(0)
    ns = hr_out.shape[1]
    nblk = bq_ref.shape[0]
    sst = ns // nblk
    rows = tt * bb
    rmm = min(S5_ROWS_MM, rows)

    @pl.when(j == 0)
    def _():
        hr_out[...] = h0r_ref[...]
        hi_out[...] = h0i_ref[...]
        wglu_s[...] = wglu_ref[...].astype(BF16)

    for n in range(nblk):
        cs = slice(n * sst, (n + 1) * sst)
        cis = slice(ns + n * sst, ns + (n + 1) * sst)
        chs = slice(n * S5_CH, (n + 1) * S5_CH)
        for r in range(rows // rmm):
            rs = slice(r * rmm, (r + 1) * rmm)
            bu = _mm(u_ref[rs, chs].astype(BF16), bq_ref[n])
            hbuf[rs, cs] = bu[:, :sst]
            hbuf[rs, cis] = bu[:, sst:]

        def row_block(rb, _):
            r0 = pl.multiple_of(rb * SUBLANES, SUBLANES)
            ar = jnp.broadcast_to(ar_ref[:, cs], (SUBLANES, sst))
            ai = jnp.broadcast_to(ai_ref[:, cs], (SUBLANES, sst))

            def step(t, carry):
                hr, hi = carry
                rr = pl.multiple_of(t * bb + r0, SUBLANES)
                nr = ar * hr - ai * hi + hbuf[pl.ds(rr, SUBLANES), cs]
                ni = ar * hi + ai * hr + hbuf[pl.ds(rr, SUBLANES), cis]
                hbuf[pl.ds(rr, SUBLANES), cs] = nr
                hbuf[pl.ds(rr, SUBLANES), cis] = ni
                return nr, ni

            hr, hi = lax.fori_loop(0, tt, step,
                                   (hr_out[pl.ds(r0, SUBLANES), cs], hi_out[pl.ds(r0, SUBLANES), cs]),
                                   unroll=min(tt, 8))
            hr_out[pl.ds(r0, SUBLANES), cs] = hr
            hi_out[pl.ds(r0, SUBLANES), cs] = hi
            return 0

        lax.fori_loop(0, bb // SUBLANES, row_block, 0)

        for r in range(rows // rmm):
            rs = slice(r * rmm, (r + 1) * rmm)
            ybuf[rs, chs] = (_mm(hbuf[rs, cs].astype(BF16), cq_ref[n, 0:sst, :])
                             + _mm(hbuf[rs, cis].astype(BF16), cq_ref[n, sst:2 * sst, :])
                             + d_ref[:, chs] * u_ref[rs, chs])

    for r in range(rows // rmm):
        rs = slice(r * rmm, (r + 1) * rmm)
        y = ybuf[rs, :]
        z = 0.5 * y * (1.0 + jnp.tanh(math.sqrt(2.0 / math.pi) * (y + 0.044715 * (y * y * y))))
        z = z * _sigmoid(_mm(z.astype(BF16), wglu_s[...]) + bglu_ref[...])
        y_ref[rs, :] = _rms(z, gn_ref[...])


def _s5(u, h0r, h0i, ar, ai, bq, cq, d, wglu, bglu, gn, layer, *, L, B):
    W = u.shape[1]
    ns = h0r.shape[1]
    tt = _tile(L, max(8, 512 // B))
    kern = functools.partial(_s5_kernel, tt=tt, bb=B)
    return pl.pallas_call(
        kern,
        grid=(L // tt,),
        in_specs=[_rows(tt * B, W), _full((B, ns)), _full((B, ns)),
                  _layer(ar, layer), _layer(ai, layer), _layer(bq, layer), _layer(cq, layer), _layer(d, layer),
                  _layer(wglu, layer), _layer(bglu, layer), _layer(gn, layer)],
        out_specs=[_rows(tt * B, W), _full((B, ns)), _full((B, ns))],
        out_shape=[jax.ShapeDtypeStruct((L * B, W), F32), jax.ShapeDtypeStruct((B, ns), F32),
                   jax.ShapeDtypeStruct((B, ns), F32)],
        scratch_shapes=[pltpu.VMEM((tt * B, 2 * ns), F32), pltpu.VMEM((tt * B, W), F32),
                        pltpu.VMEM(wglu.shape[1:], BF16)],
        compiler_params=_cparams("arbitrary"),
        name="s5",
    )(u, h0r, h0i, ar, ai, bq, cq, d, wglu, bglu, gn)


def _gla_kernel(qkv_ref, r_ref, la_ref, s0_ref, gn_ref, y_ref, s_ref, *, nseq, ls):
    c = pl.program_id(1)
    rows = nseq * ls
    kw = GLA_HEADS * GLA_DK

    @pl.when(c == 0)
    def _():
        s_ref[...] = s0_ref[...]

    ri = lax.broadcasted_iota(jnp.int32, (rows, rows), 0)
    ci = lax.broadcasted_iota(jnp.int32, (rows, rows), 1)
    causal = ri >= ci
    if nseq > 1:
        causal = causal & ((ri // ls) == (ci // ls))
    tril = jnp.where(causal, 1.0, 0.0).astype(BF16)

    la = la_ref[...].reshape(rows, kw)
    p0 = la.astype(BF16)
    r1 = la - p0.astype(F32)
    p1 = r1.astype(BF16)
    p2 = (r1 - p1.astype(F32)).astype(BF16)
    bcum = _mm(tril, p0) + _mm(tril, p1) + _mm(tril, p2)

    qkv = qkv_ref[...].reshape(rows, qkv_ref.shape[2])
    rg = r_ref[...].reshape(rows, r_ref.shape[2])
    q = qkv[:, 0:kw] * (GLA_DK ** -0.5)
    k = qkv[:, kw:2 * kw]
    qd = (q * jnp.exp(bcum)).astype(BF16)
    kd = (k * jnp.exp(-bcum)).astype(BF16)
    eye = (lax.broadcasted_iota(jnp.int32, (GLA_DK, GLA_DK), 0)
           == lax.broadcasted_iota(jnp.int32, (GLA_DK, GLA_DK), 1))

    for h in range(GLA_HEADS):
        hs = slice(h * GLA_DK, (h + 1) * GLA_DK)
        vs = slice(2 * kw + h * GLA_DV, 2 * kw + (h + 1) * GLA_DV)
        v = qkv[:, vs].astype(BF16)
        att = jnp.where(causal, _mm_nt(qd[:, hs], kd[:, hs]), 0.0)
        o_intra = _mm(att.astype(BF16), v)
        o_parts = []
        for s in range(nseq):
            rs = slice(s * ls, (s + 1) * ls)
            state = s_ref[s, h]
            o_parts.append(o_intra[rs] + _mm(qd[rs, hs], state.astype(BF16)))
            blast = bcum[(s + 1) * ls - 1:(s + 1) * ls, hs]
            k2 = (k[rs, hs] * jnp.exp(blast - bcum[rs, hs])).astype(BF16)
            dec_col = jnp.sum(jnp.where(eye, jnp.exp(blast), 0.0), axis=1, keepdims=True)
            s_ref[s, h] = dec_col * state + _mm_tn(k2, v[rs])
        o = o_parts[0] if nseq == 1 else jnp.concatenate(o_parts, axis=0)
        y = _rms(o, gn_ref[...]) * _silu(rg[:, h * GLA_DV:(h + 1) * GLA_DV])
        y_ref[:, :, h * GLA_DV:(h + 1) * GLA_DV] = y.reshape(nseq, ls, GLA_DV)


def _gla(qkv, r, la, s0, gn, layer, *, B, L):
    nseq = _tile(B, 8)
    ls = _tile(L, 64)
    nc = L // ls
    kern = functools.partial(_gla_kernel, nseq=nseq, ls=ls)
    sshape = (B, GLA_HEADS, GLA_DK, GLA_DV)
    sblk = (nseq,) + sshape[1:]
    blk3 = lambda a: pl.BlockSpec((nseq, ls, a.shape[1]), lambda b, c: (b, c, 0))
    vw = GLA_HEADS * GLA_DV
    y, s = pl.pallas_call(
        kern,
        grid=(B // nseq, nc),
        in_specs=[blk3(qkv), blk3(r), blk3(la), pl.BlockSpec(sblk, lambda b, c: (b, 0, 0, 0)), _layer(gn, layer)],
        out_specs=[pl.BlockSpec((nseq, ls, vw), lambda b, c: (b, c, 0)),
                   pl.BlockSpec(sblk, lambda b, c: (b, 0, 0, 0))],
        out_shape=[jax.ShapeDtypeStruct((B, L, vw), F32), jax.ShapeDtypeStruct(sshape, F32)],
        compiler_params=_cparams("arbitrary", "arbitrary"),
        name="gla",
    )(qkv.reshape(B, L, -1), r.reshape(B, L, -1), la.reshape(B, L, -1), s0, gn)
    return y.reshape(B * L, vw), s


def _mix_out_kernel(x_ref, a_ref, b_ref, w_ref, o_ref, w_s):
    @pl.when(_first_step(0))
    def _():
        w_s[...] = w_ref[...].astype(BF16)

    wa = a_ref.shape[1]
    o_ref[...] = (x_ref[...] + _mm(a_ref[...].astype(BF16), w_s[0:wa, :])
                  + _mm(b_ref[...].astype(BF16), w_s[wa:, :]))


def _mix_out(x, ys5, ygla, w_out, layer, *, time_major_batch):
    T, D = x.shape
    wa = w_out.shape[1] - ygla.shape[1]
    if time_major_batch:
        L = T // time_major_batch
        tm = _tile(L, 512)
        nt = L // tm
        a_spec = pl.BlockSpec((tm, wa), lambda i: (i % nt, i // nt))
    else:
        tm = _tile(T, 512)
        a_spec = _rows(tm, wa)
    return pl.pallas_call(
        _mix_out_kernel,
        grid=(T // tm,),
        in_specs=[_rows(tm, D), a_spec, _rows(tm, ygla.shape[1]), _layer(w_out, layer)],
        out_specs=_rows(tm, D),
        out_shape=jax.ShapeDtypeStruct((T, D), F32),
        scratch_shapes=[pltpu.VMEM(w_out.shape[1:], BF16)],
        compiler_params=_cparams("arbitrary"),
        name="mix_out",
    )(x, ys5, ygla, w_out)


def _norm_mm_kernel(*refs, nw):
    x_ref, g_ref = refs[0], refs[1]
    w_refs = refs[2:2 + nw]
    o_refs = refs[2 + nw:2 + 2 * nw]
    w_ss = refs[2 + 2 * nw:]

    @pl.when(_first_step(0))
    def _():
        for w_ref, w_s in zip(w_refs, w_ss):
            w_s[...] = w_ref[...].astype(BF16)

    h = _rms(x_ref[...], g_ref[...]).astype(BF16)
    for o_ref, w_s in zip(o_refs, w_ss):
        o_ref[...] = _mm(h, w_s[...])


def _norm_mm(x, g, ws, layer):
    T, D = x.shape
    tm = _tile(T, 512)
    nw = len(ws)
    outs = pl.pallas_call(
        functools.partial(_norm_mm_kernel, nw=nw),
        grid=(T // tm,),
        in_specs=[_rows(tm, D), _layer(g, layer)] + [_layer(w, layer) for w in ws],
        out_specs=[_rows(tm, w.shape[2]) for w in ws],
        out_shape=[jax.ShapeDtypeStruct((T, w.shape[2]), F32) for w in ws],
        scratch_shapes=[pltpu.VMEM(w.shape[1:], BF16) for w in ws],
        compiler_params=_cparams("arbitrary"),
        name="norm_mm",
    )(x, g, *ws)
    return outs


def _mm_res_kernel(x_ref, a_ref, w_ref, o_ref, w_s):
    @pl.when(_first_step(0))
    def _():
        w_s[...] = w_ref[...].astype(BF16)

    o_ref[...] = x_ref[...] + _mm(a_ref[...].astype(BF16), w_s[...])


def _mm_res(x, a, w, layer):
    T, D = x.shape
    tm = _tile(T, 512)
    return pl.pallas_call(
        _mm_res_kernel,
        grid=(T // tm,),
        in_specs=[_rows(tm, D), _rows(tm, a.shape[1]), _layer(w, layer)],
        out_specs=_rows(tm, D),
        out_shape=jax.ShapeDtypeStruct((T, D), F32),
        scratch_shapes=[pltpu.VMEM(w.shape[1:], BF16)],
        compiler_params=_cparams("arbitrary"),
        name="mm_res",
    )(x, a, w)


def _xattn_kernel(q_ref, k_ref, v_ref, o_ref):
    dh = q_ref.shape[1] // XA_HEADS
    scale = dh ** -0.5
    for h in range(XA_HEADS):
        hs = slice(h * dh, (h + 1) * dh)
        q = q_ref[:, hs].astype(BF16)
        k = k_ref[0, :, hs].astype(BF16)
        v = v_ref[0, :, hs].astype(BF16)
        sc = _mm_nt(q, k) * scale
        p = jnp.exp(sc - jnp.max(sc, axis=-1, keepdims=True))
        p = p / jnp.sum(p, axis=-1, keepdims=True)
        o_ref[:, hs] = _mm(p.astype(BF16), v)


def _xattn(q, mk, mv, *, B, L):
    T, D = q.shape
    M = mk.shape[1]
    lq = _tile(L, 512)
    nc = L // lq
    return pl.pallas_call(
        _xattn_kernel,
        grid=(B, nc),
        in_specs=[pl.BlockSpec((lq, D), lambda b, c: (b * nc + c, 0)),
                  pl.BlockSpec((1, M, D), lambda b, c: (b, 0, 0)),
                  pl.BlockSpec((1, M, D), lambda b, c: (b, 0, 0))],
        out_specs=pl.BlockSpec((lq, D), lambda b, c: (b * nc + c, 0)),
        out_shape=jax.ShapeDtypeStruct((T, D), F32),
        compiler_params=_cparams("arbitrary", "arbitrary"),
        name="xattn",
    )(q, mk, mv)


def _xattn_cache_kernel(q_ref, k_ref, v_ref, o_ref, *, nseq, lq, m):
    D = q_ref.shape[1]
    dh = D // XA_HEADS
    nc = dh // LANES
    hc = XA_HEADS * nc
    scale = dh ** -0.5
    npad = LANES - XA_HEADS * lq
    for s in range(nseq):
        rs = slice(s * lq, (s + 1) * lq)
        sts = []
        for h in range(XA_HEADS):
            acc = None
            for c in range(nc):
                k = k_ref[s, pl.ds(c * XA_HEADS + h, m, stride=hc), :].astype(BF16)
                qs = q_ref[rs, h * dh + c * LANES:h * dh + (c + 1) * LANES].astype(BF16)
                t = _mm_nt(k, qs)
                acc = t if acc is None else acc + t
            sts.append(acc)
        st = jnp.concatenate(sts + [jnp.zeros((m, npad), F32)], axis=1) * scale
        e = jnp.exp(st - jnp.max(st, axis=0, keepdims=True))
        p = (e / jnp.sum(e, axis=0, keepdims=True)).T
        for h in range(XA_HEADS):
            ph = p[h * lq:(h + 1) * lq, :].astype(BF16)
            for c in range(nc):
                v = v_ref[s, pl.ds(c * XA_HEADS + h, m, stride=hc), :].astype(BF16)
                o_ref[rs, h * dh + c * LANES:h * dh + (c + 1) * LANES] = _mm(ph, v)


def _xattn_cache(q, ck, cv, layer, *, B, L):
    T, D = q.shape
    depth, _, M, H, dh = ck.shape
    nc = dh // LANES

    def view(c):
        c = c.reshape(depth, B, M, H, nc, LANES).transpose(0, 1, 2, 4, 3, 5)
        return c.reshape(depth, B, M * nc * H, LANES)

    nseq = _tile(B, 4)
    kern = functools.partial(_xattn_cache_kernel, nseq=nseq, lq=L, m=M)
    cspec = pl.BlockSpec((None, nseq, M * nc * H, LANES), lambda b: (layer, b, 0, 0))
    return pl.pallas_call(
        kern,
        grid=(B // nseq,),
        in_specs=[_rows(nseq * L, D), cspec, cspec],
        out_specs=_rows(nseq * L, D),
        out_shape=jax.ShapeDtypeStruct((T, D), F32),
        compiler_params=_cparams("arbitrary"),
        name="xattn_cache",
    )(q, view(ck), view(cv))


FF_CHUNK = 256


def _swiglu_chunk(h, wg_ref, wu_ref, wd_ref):
    act = _silu(_mm(h, wg_ref[...].astype(BF16))) * _mm(h, wu_ref[...].astype(BF16))
    return _mm(act.astype(BF16), wd_ref[...].astype(BF16))


def _ffn_kernel(x_ref, g_ref, wg_ref, wu_ref, wd_ref, o_ref, h_scr, acc):
    j = pl.program_id(1)

    @pl.when(j == 0)
    def _():
        h_scr[...] = _rms(x_ref[...], g_ref[...]).astype(BF16)
        acc[...] = jnp.zeros_like(acc)

    acc[...] += _swiglu_chunk(h_scr[...], wg_ref, wu_ref, wd_ref)

    @pl.when(j == pl.num_programs(1) - 1)
    def _():
        o_ref[...] = x_ref[...] + acc[...]


def _ffn(x, g, w_gu, w_down, layer, sub):
    T, D = x.shape
    dff = w_down.shape[1]
    tm = _tile(T, 1024)
    nj = dff // FF_CHUNK
    return pl.pallas_call(
        _ffn_kernel,
        grid=(T // tm, nj),
        in_specs=[pl.BlockSpec((tm, D), lambda i, j: (i, 0)), _layer(g, layer),
                  pl.BlockSpec((None, D, FF_CHUNK), lambda i, j: (sub, 0, j)),
                  pl.BlockSpec((None, D, FF_CHUNK), lambda i, j: (sub, 0, nj + j)),
                  pl.BlockSpec((None, FF_CHUNK, D), lambda i, j: (sub, j, 0))],
        out_specs=pl.BlockSpec((tm, D), lambda i, j: (i, 0)),
        out_shape=jax.ShapeDtypeStruct((T, D), F32),
        scratch_shapes=[pltpu.VMEM((tm, D), BF16), pltpu.VMEM((tm, D), F32)],
        compiler_params=_cparams("arbitrary", "arbitrary"),
        name="ffn",
    )(x, g, w_gu, w_gu, w_down)


def _router_kernel(x_ref, g_ref, wr_ref, h_ref, idx_ref, wt_ref):
    h = _rms(x_ref[...], g_ref[...])
    h_ref[...] = h.astype(BF16)
    logits = jnp.dot(h, wr_ref[...], preferred_element_type=F32, precision=lax.Precision.HIGHEST)
    lane = lax.broadcasted_iota(jnp.int32, logits.shape, 1)
    neg = jnp.float32(-jnp.inf)
    logits = jnp.where(lane < N_EXPERTS, logits, neg)
    v1 = jnp.max(logits, axis=-1, keepdims=True)
    i1 = jnp.min(jnp.where(logits == v1, lane, LANES), axis=-1, keepdims=True)
    rest = jnp.where(lane == i1, neg, logits)
    v2 = jnp.max(rest, axis=-1, keepdims=True)
    i2 = jnp.min(jnp.where(rest == v2, lane, LANES), axis=-1, keepdims=True)
    e2 = jnp.exp(v2 - v1)
    w1 = 1.0 / (1.0 + e2)
    w2 = e2 / (1.0 + e2)
    idx_ref[...] = jnp.where(lane == 0, i1, jnp.where(lane == 1, i2, 0))
    wt_ref[...] = jnp.where(lane == 0, w1, jnp.where(lane == 1, w2, 0.0))


def _router(x, g, wr, layer):
    T, D = x.shape
    tm = _tile(T, 512)
    return pl.pallas_call(
        _router_kernel,
        grid=(T // tm,),
        in_specs=[_rows(tm, D), _layer(g, layer), _full(wr.shape)],
        out_specs=[_rows(tm, D), _rows(tm, LANES), _rows(tm, LANES)],
        out_shape=[jax.ShapeDtypeStruct((T, D), BF16), jax.ShapeDtypeStruct((T, LANES), jnp.int32),
                   jax.ShapeDtypeStruct((T, LANES), F32)],
        compiler_params=_cparams("arbitrary"),
        name="router",
    )(x, g, wr)


def _moe_kernel(te_ref, nu_ref, h_ref, gate_ref, wg_ref, wu_ref, wd_ref, o_ref, acc):
    i = pl.program_id(0)
    j = pl.program_id(1)

    @pl.when(j == 0)
    def _():
        acc[...] = jnp.zeros_like(acc)

    @pl.when(i < nu_ref[0])
    def _():
        acc[...] += _swiglu_chunk(h_ref[...], wg_ref, wu_ref, wd_ref)

    @pl.when(j == pl.num_programs(1) - 1)
    def _():
        o_ref[...] = gate_ref[...] * acc[...]


def _moe_ffn(hs, gate, tile_expert, n_used, w_gu, w_down, sub, *, tm):
    Tp, D = hs.shape
    dff = w_down.shape[2]
    nj = dff // FF_CHUNK
    grid_spec = pltpu.PrefetchScalarGridSpec(
        num_scalar_prefetch=2,
        grid=(Tp // tm, nj),
        in_specs=[pl.BlockSpec((tm, D), lambda i, j, te, nu: (i, 0)),
                  pl.BlockSpec((tm, 1), lambda i, j, te, nu: (i, 0)),
                  pl.BlockSpec((None, None, D, FF_CHUNK), lambda i, j, te, nu: (sub, te[i], 0, j)),
                  pl.BlockSpec((None, None, D, FF_CHUNK), lambda i, j, te, nu: (sub, te[i], 0, nj + j)),
                  pl.BlockSpec((None, None, FF_CHUNK, D), lambda i, j, te, nu: (sub, te[i], j, 0))],
        out_specs=pl.BlockSpec((tm, D), lambda i, j, te, nu: (i, 0)),
        scratch_shapes=[pltpu.VMEM((tm, D), F32)],
    )
    return pl.pallas_call(
        _moe_kernel,
        grid_spec=grid_spec,
        out_shape=jax.ShapeDtypeStruct((Tp, D), F32),
        compiler_params=_cparams("arbitrary", "arbitrary"),
        name="moe_ffn",
    )(tile_expert, n_used, hs, gate, w_gu, w_gu, w_down)


def _moe(xs, g, wr, w_gu, w_down, layer, sub):
    sizes = [x.shape[0] for x in xs]
    T = sum(sizes)
    tm = _tile(T, 1024)
    routed = [_router(x, g, wr, layer) for x in xs]
    h = jnp.concatenate([r[0] for r in routed], axis=0)
    idx = jnp.concatenate([r[1][:, :2] for r in routed], axis=0)
    wts = jnp.concatenate([r[2][:, :2] for r in routed], axis=0)

    flat_e = idx.reshape(-1)
    onehot = (flat_e[:, None] == jnp.arange(N_EXPERTS, dtype=jnp.int32)[None, :]).astype(jnp.int32)
    csum = jnp.cumsum(onehot, axis=0)
    rank = jnp.sum(jnp.where(onehot > 0, csum, 0), axis=1) - 1
    counts = csum[-1]
    padded = ((counts + tm - 1) // tm) * tm
    ends = jnp.cumsum(padded)
    starts = ends - padded
    dest = jnp.sum(jnp.where(onehot > 0, starts[None, :], 0), axis=1) + rank
    n_tiles = (2 * T) // tm + N_EXPERTS
    Tp = n_tiles * tm
    tile_start = jnp.arange(n_tiles, dtype=jnp.int32) * tm
    tile_expert = jnp.minimum(jnp.sum((tile_start[:, None] >= ends[None, :]).astype(jnp.int32), axis=1),
                              N_EXPERTS - 1).astype(jnp.int32)
    n_used = (ends[-1] // tm).astype(jnp.int32).reshape(1)
    src = jnp.zeros((Tp,), jnp.int32).at[dest].set(jnp.arange(2 * T, dtype=jnp.int32) // 2)
    gate = jnp.zeros((Tp,), F32).at[dest].set(wts.reshape(-1))
    hs = jnp.take(h, src, axis=0)
    y = _moe_ffn(hs, gate.reshape(Tp, 1), tile_expert, n_used, w_gu, w_down, sub, tm=tm)
    d2 = dest.reshape(T, 2)
    f = jnp.take(y, d2[:, 0], axis=0) + jnp.take(y, d2[:, 1], axis=0)
    outs, off = [], 0
    for x, n in zip(xs, sizes):
        outs.append(x + f[off:off + n])
        off += n
    return outs


def _norm_kernel(x_ref, g_ref, o_ref):
    o_ref[...] = _rms(x_ref[...], g_ref[...])


def _norm(x, g):
    T, D = x.shape
    tm = _tile(T, 1024)
    return pl.pallas_call(
        _norm_kernel,
        grid=(T // tm,),
        in_specs=[_rows(tm, D), _full(g.shape)],
        out_specs=_rows(tm, D),
        out_shape=jax.ShapeDtypeStruct((T, D), F32),
        compiler_params=_cparams("arbitrary"),
        name="final_norm",
    )(x, g)


def _s5_block_weights(coef_re, coef_im, b_re, b_im, c_re, c_im):
    depth, G, P, C = b_re.shape
    gpb = S5_CH // C
    nblk = G // gpb
    sst = gpb * P
    bbar_re = coef_re[..., None] * b_re - coef_im[..., None] * b_im
    bbar_im = coef_re[..., None] * b_im + coef_im[..., None] * b_re
    eye = jnp.eye(gpb, dtype=F32)

    def bblk(t):
        t = t.reshape(depth, nblk, gpb, P, C)
        return jnp.einsum('dnkpc,kj->dnkcjp', t, eye).reshape(depth, nblk, S5_CH, sst)

    def cblk(t):
        t = t.reshape(depth, nblk, gpb, C, P)
        return jnp.einsum('dnkcp,kj->dnkpjc', t, eye).reshape(depth, nblk, sst, S5_CH)

    bq = jnp.concatenate([bblk(bbar_re), bblk(bbar_im)], axis=3).astype(BF16)
    cq = jnp.concatenate([cblk(c_re), cblk(-c_im)], axis=2).astype(BF16)
    return bq, cq


def kernel(x_prompt, x_sample, mem_prompt, cache_mem_k, cache_mem_v, state_s5_re, state_s5_im, state_gla,
           norm_mix, w_in, s5_lambda_re, s5_lambda_im, s5_b_re, s5_b_im, s5_c_re, s5_c_im, s5_d, s5_log_dt,
           s5_w_glu, s5_b_glu, s5_norm, gla_w_gate2, gla_b_gate, gla_norm, w_out,
           norm_xa, norm_mem, xa_wq, xa_wk, xa_wv, xa_wo,
           norm_ffn, ffn_w_gu, ffn_w_down, moe_router, moe_w_gu, moe_w_down, norm_final):
    Bp, Lp, D = x_prompt.shape
    Bs, Ls, _ = x_sample.shape
    depth = w_in.shape[0]
    M = mem_prompt.shape[1]
    G, P = s5_lambda_re.shape[1:]
    ns = G * P
    s5w = G * S5_GROUP
    kw = GLA_HEADS * GLA_DK
    vw = GLA_HEADS * GLA_DV

    xp = x_prompt.reshape(Bp * Lp, D)
    xs = x_sample.reshape(Bs * Ls, D)
    mem = mem_prompt.reshape(Bp * M, D)
    vec = lambda v: v.reshape(depth, 1, -1)
    norm_mix, norm_xa, norm_mem, norm_ffn = vec(norm_mix), vec(norm_xa), vec(norm_mem), vec(norm_ffn)
    s5_b_glu, s5_norm, gla_b_gate, gla_norm, s5_d = (vec(s5_b_glu), vec(s5_norm), vec(gla_b_gate), vec(gla_norm),
                                                     vec(s5_d))
    zeros_s5 = jnp.zeros((Bp, ns), F32)
    zeros_gla = jnp.zeros((Bp, GLA_HEADS, GLA_DK, GLA_DV), F32)

    ar, ai, cfr, cfi = _s5_disc(s5_lambda_re.reshape(depth * G, P), s5_lambda_im.reshape(depth * G, P),
                                s5_log_dt.reshape(depth * G, 1))
    bq, cq = _s5_block_weights(cfr.reshape(depth, G, P), cfi.reshape(depth, G, P), s5_b_re, s5_b_im, s5_c_re, s5_c_im)
    ar = ar.reshape(depth, 1, ns)
    ai = ai.reshape(depth, 1, ns)

    outs = {k: [] for k in ('mk', 'mv', 'pr', 'pi', 'pg', 'sr', 'si', 'sg')}
    for i in range(depth):
        mk, mv = _norm_mm(mem, norm_mem, [xa_wk, xa_wv], i)
        mk = mk.reshape(Bp, M, D)
        mv = mv.reshape(Bp, M, D)
        outs['mk'].append(mk.reshape(Bp, M, XA_HEADS, D // XA_HEADS))
        outs['mv'].append(mv.reshape(Bp, M, XA_HEADS, D // XA_HEADS))

        new_x = []
        for grp, x, B, L in (('p', xp, Bp, Lp), ('s', xs, Bs, Ls)):
            prompt = grp == 'p'
            u, qkv, r, la = _mix_in(x, norm_mix, w_in, gla_w_gate2, gla_b_gate, i, su=s5w, sqkv=2 * kw + vw, sr=vw,
                                    time_major_batch=B if prompt else 0)
            if prompt:
                u_tm = u.reshape(L * B, s5w)
                h0r, h0i, s0 = zeros_s5, zeros_s5, zeros_gla
            else:
                u_tm = u.reshape(B, L, s5w).transpose(1, 0, 2).reshape(L * B, s5w)
                h0r = state_s5_re[i].reshape(B, ns)
                h0i = state_s5_im[i].reshape(B, ns)
                s0 = state_gla[i]
            y5, hr, hi = _s5(u_tm, h0r, h0i, ar, ai, bq, cq, s5_d, s5_w_glu, s5_b_glu, s5_norm, i, L=L, B=B)
            if prompt:
                y5 = y5.reshape(L, B * s5w)
            else:
                y5 = y5.reshape(L, B, s5w).transpose(1, 0, 2).reshape(B * L, s5w)
            yg, sg = _gla(qkv, r, la, s0, gla_norm, i, B=B, L=L)
            outs[grp + 'r'].append(hr.reshape(B, G, P))
            outs[grp + 'i'].append(hi.reshape(B, G, P))
            outs[grp + 'g'].append(sg)
            x = _mix_out(x, y5, yg, w_out, i, time_major_batch=B if prompt else 0)
            (q,) = _norm_mm(x, norm_xa, [xa_wq], i)
            if prompt:
                o = _xattn(q, mk, mv, B=B, L=L)
            else:
                o = _xattn_cache(q, cache_mem_k, cache_mem_v, i, B=B, L=L)
            x = _mm_res(x, o, xa_wo, i)
            new_x.append(x)
        xp, xs = new_x
        if i % 2 == 0:
            xp = _ffn(xp, norm_ffn, ffn_w_gu, ffn_w_down, i, i // 2)
            xs = _ffn(xs, norm_ffn, ffn_w_gu, ffn_w_down, i, i // 2)
        else:
            wr = jnp.pad(moe_router[i // 2], ((0, 0), (0, LANES - N_EXPERTS)))
            xp, xs = _moe([xp, xs], norm_ffn, wr, moe_w_gu, moe_w_down, i, i // 2)

    y_prompt = _norm(xp, norm_final.reshape(1, D)).reshape(Bp, Lp, D)
    y_sample = _norm(xs, norm_final.reshape(1, D)).reshape(Bs, Ls, D)
    st = jnp.stack
    return (y_prompt, y_sample, st(outs['mk']), st(outs['mv']), st(outs['pr']), st(outs['pi']), st(outs['pg']),
            st(outs['sr']), st(outs['si']), st(outs['sg']))
```

```python
import functools
import math

import jax
import jax.numpy as jnp
from jax import lax
from jax.experimental import pallas as pl
from jax.experimental.pallas import tpu as pltpu

F32 = jnp.float32
BF16 = jnp.bfloat16
EPS = 1e-6

S5_GROUP = 16
GLA_HEADS = 4
GLA_DK = 64
GLA_DV = 128
GLA_GATE_TAU = 16.0
XA_HEADS = 4
N_EXPERTS = 8
LANES = 128
SUBLANES = 8
VMEM_LIMIT = 56 * 1024 * 1024


def _cparams(*sem):
    return pltpu.CompilerParams(dimension_semantics=sem, vmem_limit_bytes=VMEM_LIMIT)


def _tile(n, pref):
    t = min(n, pref)
    while n % t:
        t //= 2
    return t


def _rms(x, g):
    return x * lax.rsqrt(jnp.mean(x * x, axis=-1, keepdims=True) + EPS) * g


def _mm(a, b):
    return jnp.dot(a, b, preferred_element_type=F32)


def _mm_nt(a, b):
    return lax.dot_general(a, b, (((1,), (1,)), ((), ())), preferred_element_type=F32)


def _mm_tn(a, b):
    return lax.dot_general(a, b, (((0,), (0,)), ((), ())), preferred_element_type=F32)


def _sigmoid(x):
    return 1.0 / (1.0 + jnp.exp(-x))


def _silu(x):
    return x * _sigmoid(x)


def _full(shape):
    return pl.BlockSpec(shape, lambda *_: (0,) * len(shape))


def _rows(tm, width):
    return pl.BlockSpec((tm, width), lambda i, *_: (i, 0))


def _layer(arr, layer):
    nd = arr.ndim - 1
    return pl.BlockSpec((None,) + arr.shape[1:], lambda *_: (layer,) + (0,) * nd, pipeline_mode=pl.Buffered(1))


def _first_step(*axes):
    cond = pl.program_id(axes[0]) == 0
    for a in axes[1:]:
        cond = cond & (pl.program_id(a) == 0)
    return cond


def _mix_in_kernel(x_ref, g_ref, w_ref, wg2_ref, bg_ref, u_ref, qkv_ref, r_ref, la_ref,
                   wu_s, wqkv_s, wgl_s, wr_s, wg2_s, *, su, sqkv, rank):
    @pl.when(_first_step(0))
    def _():
        wu_s[...] = w_ref[:, 0:su].astype(BF16)
        wqkv_s[...] = w_ref[:, su:su + sqkv].astype(BF16)
        wgl_s[...] = w_ref[:, su + sqkv:su + sqkv + LANES].astype(BF16)
        wr_s[...] = w_ref[:, su + sqkv + rank:].astype(BF16)
        wg2_s[...] = jnp.zeros_like(wg2_s)
        wg2_s[0:rank, :] = wg2_ref[...].astype(BF16)

    h = _rms(x_ref[...], g_ref[...]).astype(BF16)
    u_ref[...] = _mm(h, wu_s[...])
    qkv_ref[...] = _mm(h, wqkv_s[...])
    r_ref[...] = _mm(h, wr_s[...])
    glr = _mm(h, wgl_s[...])
    pre = _mm(glr.astype(BF16), wg2_s[...]) + bg_ref[...]
    logsig = jnp.minimum(pre, 0.0) - jnp.log(1.0 + jnp.exp(-jnp.abs(pre)))
    la_ref[...] = logsig * (1.0 / GLA_GATE_TAU)


def _mix_in(x, g, w_in, wg2, bg, layer, *, su, sqkv, sr, time_major_batch):
    T, D = x.shape
    rank = wg2.shape[1]
    kw = wg2.shape[2]
    if time_major_batch:
        L = T // time_major_batch
        tm = _tile(L, 512)
        nt = L // tm
        u_shape = jax.ShapeDtypeStruct((L, time_major_batch * su), F32)
        u_spec = pl.BlockSpec((tm, su), lambda i: (i % nt, i // nt))
    else:
        tm = _tile(T, 512)
        u_shape = jax.ShapeDtypeStruct((T, su), F32)
        u_spec = _rows(tm, su)
    kern = functools.partial(_mix_in_kernel, su=su, sqkv=sqkv, rank=rank)
    return pl.pallas_call(
        kern,
        grid=(T // tm,),
        in_specs=[_rows(tm, D), _layer(g, layer), _layer(w_in, layer), _layer(wg2, layer), _layer(bg, layer)],
        out_specs=[u_spec, _rows(tm, sqkv), _rows(tm, sr), _rows(tm, kw)],
        out_shape=[u_shape, jax.ShapeDtypeStruct((T, sqkv), F32), jax.ShapeDtypeStruct((T, sr), F32),
                   jax.ShapeDtypeStruct((T, kw), F32)],
        scratch_shapes=[pltpu.VMEM((D, su), BF16), pltpu.VMEM((D, sqkv), BF16), pltpu.VMEM((D, LANES), BF16),
                        pltpu.VMEM((D, sr), BF16), pltpu.VMEM((LANES, kw), BF16)],
        compiler_params=_cparams("arbitrary"),
        name="mix_in",
    )(x, g, w_in, wg2, bg)


def _s5_disc_kernel(lr_ref, li_ref, ldt_ref, ar_ref, ai_ref, cr_ref, ci_ref):
    lam_re = jnp.minimum(lr_ref[...], -1e-4)
    lam_im = li_ref[...]
    dt = jnp.exp(ldt_ref[...])
    mag = jnp.exp(lam_re * dt)
    ang = lam_im * dt
    abar_re = mag * jnp.cos(ang)
    abar_im = mag * jnp.sin(ang)
    den = lam_re * lam_re + lam_im * lam_im
    n_re = abar_re - 1.0
    n_im = abar_im
    ar_ref[...] = abar_re
    ai_ref[...] = abar_im
    cr_ref[...] = (n_re * lam_re + n_im * lam_im) / den
    ci_ref[...] = (n_im * lam_re - n_re * lam_im) / den


def _s5_disc(lam_re, lam_im, log_dt):
    shp = lam_re.shape
    out = jax.ShapeDtypeStruct(shp, F32)
    return pl.pallas_call(
        _s5_disc_kernel,
        in_specs=[_full(shp), _full(shp), _full(log_dt.shape)],
        out_specs=[_full(shp)] * 4,
        out_shape=[out] * 4,
        name="s5_disc",
    )(lam_re, lam_im, log_dt)


S5_CH = 128
S5_ROWS_MM = 256


def _s5_kernel(u_ref, h0r_ref, h0i_ref, ar_ref, ai_ref, bq_ref, cq_ref, d_ref, wglu_ref, bglu_ref, gn_ref,
               y_ref, hr_out, hi_out, hbuf, ybuf, wglu_s, *, tt, bb):
    j = pl.program_id(0)
    ns = hr_out.shape[1]
    nblk = bq_ref.shape[0]
    sst = ns // nblk
    rows = tt * bb
    rmm = min(S5_ROWS_MM, rows)

    @pl.when(j == 0)
    def _():
        hr_out[...] = h0r_ref[...]
        hi_out[...] = h0i_ref[...]
        wglu_s[...] = wglu_ref[...].astype(BF16)

    for n in range(nblk):
        cs = slice(n * sst, (n + 1) * sst)
        cis = slice(ns + n * sst, ns + (n + 1) * sst)
        chs = slice(n * S5_CH, (n + 1) * S5_CH)
        for r in range(rows // rmm):
            rs = slice(r * rmm, (r + 1) * rmm)
            bu = _mm(u_ref[rs, chs].astype(BF16), bq_ref[n])
            hbuf[rs, cs] = bu[:, :sst]
            hbuf[rs, cis] = bu[:, sst:]

        def row_block(rb, _):
            r0 = pl.multiple_of(rb * SUBLANES, SUBLANES)
            ar = jnp.broadcast_to(ar_ref[:, cs], (SUBLANES, sst))
            ai = jnp.broadcast_to(ai_ref[:, cs], (SUBLANES, sst))

            def step(t, carry):
                hr, hi = carry
                rr = pl.multiple_of(t * bb + r0, SUBLANES)
                nr = ar * hr - ai * hi + hbuf[pl.ds(rr, SUBLANES), cs]
                ni = ar * hi + ai * hr + hbuf[pl.ds(rr, SUBLANES), cis]
                hbuf[pl.ds(rr, SUBLANES), cs] = nr
                hbuf[pl.ds(rr, SUBLANES), cis] = ni
                return nr, ni

            hr, hi = lax.fori_loop(0, tt, step,
                                   (hr_out[pl.ds(r0, SUBLANES), cs], hi_out[pl.ds(r0, SUBLANES), cs]),
                                   unroll=min(tt, 8))
            hr_out[pl.ds(r0, SUBLANES), cs] = hr
            hi_out[pl.ds(r0, SUBLANES), cs] = hi
            return 0

        lax.fori_loop(0, bb // SUBLANES, row_block, 0)

        for r in range(rows // rmm):
            rs = slice(r * rmm, (r + 1) * rmm)
            ybuf[rs, chs] = (_mm(hbuf[rs, cs].astype(BF16), cq_ref[n, 0:sst, :])
                             + _mm(hbuf[rs, cis].astype(BF16), cq_ref[n, sst:2 * sst, :])
                             + d_ref[:, chs] * u_ref[rs, chs])

    for r in range(rows // rmm):
        rs = slice(r * rmm, (r + 1) * rmm)
        y = ybuf[rs, :]
        z = 0.5 * y * (1.0 + jnp.tanh(math.sqrt(2.0 / math.pi) * (y + 0.044715 * (y * y * y))))
        z = z * _sigmoid(_mm(z.astype(BF16), wglu_s[...]) + bglu_ref[...])
        y_ref[rs, :] = _rms(z, gn_ref[...])


def _s5(u, h0r, h0i, ar, ai, bq, cq, d, wglu, bglu, gn, layer, *, L, B):
    W = u.shape[1]
    ns = h0r.shape[1]
    tt = _tile(L, max(8, 512 // B))
    kern = functools.partial(_s5_kernel, tt=tt, bb=B)
    return pl.pallas_call(
        kern,
        grid=(L // tt,),
        in_specs=[_rows(tt * B, W), _full((B, ns)), _full((B, ns)),
                  _layer(ar, layer), _layer(ai, layer), _layer(bq, layer), _layer(cq, layer), _layer(d, layer),
                  _layer(wglu, layer), _layer(bglu, layer), _layer(gn, layer)],
        out_specs=[_rows(tt * B, W), _full((B, ns)), _full((B, ns))],
        out_shape=[jax.ShapeDtypeStruct((L * B, W), F32), jax.ShapeDtypeStruct((B, ns), F32),
                   jax.ShapeDtypeStruct((B, ns), F32)],
        scratch_shapes=[pltpu.VMEM((tt * B, 2 * ns), F32), pltpu.VMEM((tt * B, W), F32),
                        pltpu.VMEM(wglu.shape[1:], BF16)],
        compiler_params=_cparams("arbitrary"),
        name="s5",
    )(u, h0r, h0i, ar, ai, bq, cq, d, wglu, bglu, gn)


def _gla_kernel(qkv_ref, r_ref, la_ref, s0_ref, gn_ref, y_ref, s_ref, *, nseq, ls):
    c = pl.program_id(1)
    rows = nseq * ls
    kw = GLA_HEADS * GLA_DK

    @pl.when(c == 0)
    def _():
        s_ref[...] = s0_ref[...]

    ri = lax.broadcasted_iota(jnp.int32, (rows, rows), 0)
    ci = lax.broadcasted_iota(jnp.int32, (rows, rows), 1)
    causal = ri >= ci
    if nseq > 1:
        causal = causal & ((ri // ls) == (ci // ls))
    tril = jnp.where(causal, 1.0, 0.0).astype(BF16)

    la = la_ref[...].reshape(rows, kw)
    p0 = la.astype(BF16)
    r1 = la - p0.astype(F32)
    p1 = r1.astype(BF16)
    p2 = (r1 - p1.astype(F32)).astype(BF16)
    bcum = _mm(tril, p0) + _mm(tril, p1) + _mm(tril, p2)

    qkv = qkv_ref[...].reshape(rows, qkv_ref.shape[2])
    rg = r_ref[...].reshape(rows, r_ref.shape[2])
    q = qkv[:, 0:kw] * (GLA_DK ** -0.5)
    k = qkv[:, kw:2 * kw]
    qd = (q * jnp.exp(bcum)).astype(BF16)
    kd = (k * jnp.exp(-bcum)).astype(BF16)
    eye = (lax.broadcasted_iota(jnp.int32, (GLA_DK, GLA_DK), 0)
           == lax.broadcasted_iota(jnp.int32, (GLA_DK, GLA_DK), 1))

    for h in range(GLA_HEADS):
        hs = slice(h * GLA_DK, (h + 1) * GLA_DK)
        vs = slice(2 * kw + h * GLA_DV, 2 * kw + (h + 1) * GLA_DV)
        v = qkv[:, vs].astype(BF16)
        att = jnp.where(causal, _mm_nt(qd[:, hs], kd[:, hs]), 0.0)
        o_intra = _mm(att.astype(BF16), v)
        o_parts = []
        for s in range(nseq):
            rs = slice(s * ls, (s + 1) * ls)
            state = s_ref[s, h]
            o_parts.append(o_intra[rs] + _mm(qd[rs, hs], state.astype(BF16)))
            blast = bcum[(s + 1) * ls - 1:(s + 1) * ls, hs]
            k2 = (k[rs, hs] * jnp.exp(blast - bcum[rs, hs])).astype(BF16)
            dec_col = jnp.sum(jnp.where(eye, jnp.exp(blast), 0.0), axis=1, keepdims=True)
            s_ref[s, h] = dec_col * state + _mm_tn(k2, v[rs])
        o = o_parts[0] if nseq == 1 else jnp.concatenate(o_parts, axis=0)
        y = _rms(o, gn_ref[...]) * _silu(rg[:, h * GLA_DV:(h + 1) * GLA_DV])
        y_ref[:, :, h * GLA_DV:(h + 1) * GLA_DV] = y.reshape(nseq, ls, GLA_DV)


def _gla(qkv, r, la, s0, gn, layer, *, B, L):
    nseq = _tile(B, 8)
    ls = _tile(L, 64)
    nc = L // ls
    kern = functools.partial(_gla_kernel, nseq=nseq, ls=ls)
    sshape = (B, GLA_HEADS, GLA_DK, GLA_DV)
    sblk = (nseq,) + sshape[1:]
    blk3 = lambda a: pl.BlockSpec((nseq, ls, a.shape[1]), lambda b, c: (b, c, 0))
    vw = GLA_HEADS * GLA_DV
    if s0.ndim == 5:
        s0_spec = pl.BlockSpec((None,) + sblk, lambda b, c: (layer, b, 0, 0, 0))
    else:
        s0_spec = pl.BlockSpec(sblk, lambda b, c: (b, 0, 0, 0))
    y, s = pl.pallas_call(
        kern,
        grid=(B // nseq, nc),
        in_specs=[blk3(qkv), blk3(r), blk3(la), s0_spec, _layer(gn, layer)],
        out_specs=[pl.BlockSpec((nseq, ls, vw), lambda b, c: (b, c, 0)),
                   pl.BlockSpec(sblk, lambda b, c: (b, 0, 0, 0))],
        out_shape=[jax.ShapeDtypeStruct((B, L, vw), F32), jax.ShapeDtypeStruct(sshape, F32)],
        compiler_params=_cparams("arbitrary", "arbitrary"),
        name="gla",
    )(qkv.reshape(B, L, -1), r.reshape(B, L, -1), la.reshape(B, L, -1), s0, gn)
    return y.reshape(B * L, vw), s


def _mix_out_kernel(x_ref, a_ref, b_ref, w_ref, o_ref, w_s):
    @pl.when(_first_step(0))
    def _():
        w_s[...] = w_ref[...].astype(BF16)

    wa = a_ref.shape[1]
    o_ref[...] = (x_ref[...] + _mm(a_ref[...].astype(BF16), w_s[0:wa, :])
                  + _mm(b_ref[...].astype(BF16), w_s[wa:, :]))


def _mix_out(x, ys5, ygla, w_out, layer, *, time_major_batch):
    T, D = x.shape
    wa = w_out.shape[1] - ygla.shape[1]
    if time_major_batch:
        L = T // time_major_batch
        tm = _tile(L, 512)
        nt = L // tm
        a_spec = pl.BlockSpec((tm, wa), lambda i: (i % nt, i // nt))
    else:
        tm = _tile(T, 512)
        a_spec = _rows(tm, wa)
    return pl.pallas_call(
        _mix_out_kernel,
        grid=(T // tm,),
        in_specs=[_rows(tm, D), a_spec, _rows(tm, ygla.shape[1]), _layer(w_out, layer)],
        out_specs=_rows(tm, D),
        out_shape=jax.ShapeDtypeStruct((T, D), F32),
        scratch_shapes=[pltpu.VMEM(w_out.shape[1:], BF16)],
        compiler_params=_cparams("arbitrary"),
        name="mix_out",
    )(x, ys5, ygla, w_out)


def _norm_mm_kernel(*refs, nw):
    x_ref, g_ref = refs[0], refs[1]
    w_refs = refs[2:2 + nw]
    o_refs = refs[2 + nw:2 + 2 * nw]
    w_ss = refs[2 + 2 * nw:]

    @pl.when(_first_step(0))
    def _():
        for w_ref, w_s in zip(w_refs, w_ss):
            w_s[...] = w_ref[...].astype(BF16)

    h = _rms(x_ref[...], g_ref[...]).astype(BF16)
    for o_ref, w_s in zip(o_refs, w_ss):
        o_ref[...] = _mm(h, w_s[...])


def _norm_mm(x, g, ws, layer):
    T, D = x.shape
    tm = _tile(T, 512)
    nw = len(ws)
    outs = pl.pallas_call(
        functools.partial(_norm_mm_kernel, nw=nw),
        grid=(T // tm,),
        in_specs=[_rows(tm, D), _layer(g, layer)] + [_layer(w, layer) for w in ws],
        out_specs=[_rows(tm, w.shape[2]) for w in ws],
        out_shape=[jax.ShapeDtypeStruct((T, w.shape[2]), F32) for w in ws],
        scratch_shapes=[pltpu.VMEM(w.shape[1:], BF16) for w in ws],
        compiler_params=_cparams("arbitrary"),
        name="norm_mm",
    )(x, g, *ws)
    return outs


def _mm_res_kernel(x_ref, a_ref, w_ref, o_ref, w_s):
    @pl.when(_first_step(0))
    def _():
        w_s[...] = w_ref[...].astype(BF16)

    o_ref[...] = x_ref[...] + _mm(a_ref[...].astype(BF16), w_s[...])


def _mm_res(x, a, w, layer):
    T, D = x.shape
    tm = _tile(T, 512)
    return pl.pallas_call(
        _mm_res_kernel,
        grid=(T // tm,),
        in_specs=[_rows(tm, D), _rows(tm, a.shape[1]), _layer(w, layer)],
        out_specs=_rows(tm, D),
        out_shape=jax.ShapeDtypeStruct((T, D), F32),
        scratch_shapes=[pltpu.VMEM(w.shape[1:], BF16)],
        compiler_params=_cparams("arbitrary"),
        name="mm_res",
    )(x, a, w)


def _xattn_kernel(q_ref, k_ref, v_ref, o_ref):
    dh = q_ref.shape[1] // XA_HEADS
    scale = dh ** -0.5
    for h in range(XA_HEADS):
        hs = slice(h * dh, (h + 1) * dh)
        q = q_ref[:, hs].astype(BF16)
        k = k_ref[0, :, hs].astype(BF16)
        v = v_ref[0, :, hs].astype(BF16)
        sc = _mm_nt(q, k) * scale
        p = jnp.exp(sc - jnp.max(sc, axis=-1, keepdims=True))
        p = p / jnp.sum(p, axis=-1, keepdims=True)
        o_ref[:, hs] = _mm(p.astype(BF16), v)


def _xattn(q, mk, mv, *, B, L):
    T, D = q.shape
    M = mk.shape[1]
    lq = _tile(L, 512)
    nc = L // lq
    return pl.pallas_call(
        _xattn_kernel,
        grid=(B, nc),
        in_specs=[pl.BlockSpec((lq, D), lambda b, c: (b * nc + c, 0)),
                  pl.BlockSpec((1, M, D), lambda b, c: (b, 0, 0)),
                  pl.BlockSpec((1, M, D), lambda b, c: (b, 0, 0))],
        out_specs=pl.BlockSpec((lq, D), lambda b, c: (b * nc + c, 0)),
        out_shape=jax.ShapeDtypeStruct((T, D), F32),
        compiler_params=_cparams("arbitrary", "arbitrary"),
        name="xattn",
    )(q, mk, mv)


def _xattn_cache_kernel(q_ref, k_ref, v_ref, o_ref, *, nseq, lq, m):
    D = q_ref.shape[1]
    dh = D // XA_HEADS
    nc = dh // LANES
    hc = XA_HEADS * nc
    scale = dh ** -0.5
    npad = LANES - XA_HEADS * lq
    for s in range(nseq):
        rs = slice(s * lq, (s + 1) * lq)
        sts = []
        for h in range(XA_HEADS):
            acc = None
            for c in range(nc):
                k = k_ref[s, pl.ds(c * XA_HEADS + h, m, stride=hc), :].astype(BF16)
                qs = q_ref[rs, h * dh + c * LANES:h * dh + (c + 1) * LANES].astype(BF16)
                t = _mm_nt(k, qs)
                acc = t if acc is None else acc + t
            sts.append(acc)
        st = jnp.concatenate(sts + [jnp.zeros((m, npad), F32)], axis=1) * scale
        e = jnp.exp(st - jnp.max(st, axis=0, keepdims=True))
        p = (e / jnp.sum(e, axis=0, keepdims=True)).T
        for h in range(XA_HEADS):
            ph = p[h * lq:(h + 1) * lq, :].astype(BF16)
            for c in range(nc):
                v = v_ref[s, pl.ds(c * XA_HEADS + h, m, stride=hc), :].astype(BF16)
                o_ref[rs, h * dh + c * LANES:h * dh + (c + 1) * LANES] = _mm(ph, v)


def _xattn_cache(q, ck, cv, layer, *, B, L):
    T, D = q.shape
    depth, _, M, H, dh = ck.shape
    nc = dh // LANES

    def view(c):
        c = c.reshape(depth, B, M, H, nc, LANES).transpose(0, 1, 2, 4, 3, 5)
        return c.reshape(depth, B, M * nc * H, LANES)

    nseq = _tile(B, 4)
    kern = functools.partial(_xattn_cache_kernel, nseq=nseq, lq=L, m=M)
    cspec = pl.BlockSpec((None, nseq, M * nc * H, LANES), lambda b: (layer, b, 0, 0))
    return pl.pallas_call(
        kern,
        grid=(B // nseq,),
        in_specs=[_rows(nseq * L, D), cspec, cspec],
        out_specs=_rows(nseq * L, D),
        out_shape=jax.ShapeDtypeStruct((T, D), F32),
        compiler_params=_cparams("arbitrary"),
        name="xattn_cache",
    )(q, view(ck), view(cv))


FF_CHUNK = 256


def _swiglu_chunk(h, wg_ref, wu_ref, wd_ref):
    act = _silu(_mm(h, wg_ref[...].astype(BF16))) * _mm(h, wu_ref[...].astype(BF16))
    return _mm(act.astype(BF16), wd_ref[...].astype(BF16))


def _swiglu_step(j, nchunks, h_scr, w_refs, acc):
    wga, wua, wda, wgb, wub, wdb = w_refs

    @pl.when(2 * j + 1 < nchunks)
    def _():
        h = h_scr[...]
        acc[...] += _swiglu_chunk(h, wga, wua, wda) + _swiglu_chunk(h, wgb, wub, wdb)

    @pl.when(2 * j + 1 >= nchunks)
    def _():
        acc[...] += _swiglu_chunk(h_scr[...], wga, wua, wda)


def _swiglu_specs(nchunks, D, lead, lead_none):
    second = lambda j: jnp.minimum(2 * j + 1, nchunks - 1)
    specs = []
    for blk in (lambda j: 2 * j, second):
        specs += [
            pl.BlockSpec(lead_none + (D, FF_CHUNK), lambda i, j, *p, blk=blk: lead(i, j, *p) + (0, blk(j))),
            pl.BlockSpec(lead_none + (D, FF_CHUNK), lambda i, j, *p, blk=blk: lead(i, j, *p) + (0, nchunks + blk(j))),
            pl.BlockSpec(lead_none + (FF_CHUNK, D), lambda i, j, *p, blk=blk: lead(i, j, *p) + (blk(j), 0))]
    return specs


def _ffn_kernel(x_ref, g_ref, *refs, nchunks):
    w_refs, (o_ref, h_scr, acc) = refs[:6], refs[6:]
    j = pl.program_id(1)

    @pl.when(j == 0)
    def _():
        h_scr[...] = _rms(x_ref[...], g_ref[...]).astype(BF16)
        acc[...] = jnp.zeros_like(acc)

    _swiglu_step(j, nchunks, h_scr, w_refs, acc)

    @pl.when(j == pl.num_programs(1) - 1)
    def _():
        o_ref[...] = x_ref[...] + acc[...]


def _ffn(x, g, w_gu, w_down, layer, sub):
    T, D = x.shape
    dff = w_down.shape[1]
    tm = _tile(T, 1024)
    nchunks = dff // FF_CHUNK
    return pl.pallas_call(
        functools.partial(_ffn_kernel, nchunks=nchunks),
        grid=(T // tm, (nchunks + 1) // 2),
        in_specs=[pl.BlockSpec((tm, D), lambda i, j: (i, 0)), _layer(g, layer)]
        + _swiglu_specs(nchunks, D, lambda i, j: (sub,), (None,)),
        out_specs=pl.BlockSpec((tm, D), lambda i, j: (i, 0)),
        out_shape=jax.ShapeDtypeStruct((T, D), F32),
        scratch_shapes=[pltpu.VMEM((tm, D), BF16), pltpu.VMEM((tm, D), F32)],
        compiler_params=_cparams("arbitrary", "arbitrary"),
        name="ffn",
    )(x, g, w_gu, w_gu, w_down, w_gu, w_gu, w_down)


def _router_kernel(x_ref, g_ref, wr_ref, cnt0_ref, idx_ref, wt_ref, cnt_ref):
    @pl.when(_first_step(0))
    def _():
        cnt_ref[...] = cnt0_ref[...]

    tm = x_ref.shape[0]
    h = _rms(x_ref[...], g_ref[...])
    logits = jnp.dot(h, wr_ref[...], preferred_element_type=F32, precision=lax.Precision.HIGHEST)
    lane = lax.broadcasted_iota(jnp.int32, logits.shape, 1)
    neg = jnp.float32(-jnp.inf)
    logits = jnp.where(lane < N_EXPERTS, logits, neg)
    v1 = jnp.max(logits, axis=-1, keepdims=True)
    i1 = jnp.min(jnp.where(logits == v1, lane, LANES), axis=-1, keepdims=True)
    rest = jnp.where(lane == i1, neg, logits)
    v2 = jnp.max(rest, axis=-1, keepdims=True)
    i2 = jnp.min(jnp.where(rest == v2, lane, LANES), axis=-1, keepdims=True)
    e2 = jnp.exp(v2 - v1)
    w1 = 1.0 / (1.0 + e2)
    w2 = e2 / (1.0 + e2)
    ri = lax.broadcasted_iota(jnp.int32, (tm, tm), 0)
    ci = lax.broadcasted_iota(jnp.int32, (tm, tm), 1)
    tril = jnp.where(ri >= ci, 1.0, 0.0).astype(BF16)
    cum1 = _mm(tril, jnp.where(lane == i1, 1.0, 0.0).astype(BF16))
    cum2 = _mm(tril, jnp.where(lane == i2, 1.0, 0.0).astype(BF16))
    base = cnt_ref[0:1, :]
    tot1 = cum1[tm - 1:tm, :]
    tot2 = cum2[tm - 1:tm, :]
    r1 = jnp.sum(jnp.where(lane == i1, base + cum1, 0.0), axis=-1, keepdims=True) - 1.0
    r2 = jnp.sum(jnp.where(lane == i2, base + tot1 + cum2, 0.0), axis=-1, keepdims=True) - 1.0
    cnt_ref[...] = jnp.broadcast_to(base + tot1 + tot2, cnt_ref.shape)
    idx_ref[...] = jnp.where(lane == 0, i1, jnp.where(lane == 1, i2, jnp.where(
        lane == 2, r1.astype(jnp.int32), jnp.where(lane == 3, r2.astype(jnp.int32), 0))))
    wt_ref[...] = jnp.where(lane == 0, w1, jnp.where(lane == 1, w2, 0.0))


def _router(x, g, wr, cnt0, layer):
    T, D = x.shape
    tm = _tile(T, 512)
    return pl.pallas_call(
        _router_kernel,
        grid=(T // tm,),
        in_specs=[_rows(tm, D), _layer(g, layer), _full(wr.shape), _full(cnt0.shape)],
        out_specs=[_rows(tm, LANES), _rows(tm, LANES), _full(cnt0.shape)],
        out_shape=[jax.ShapeDtypeStruct((T, LANES), jnp.int32), jax.ShapeDtypeStruct((T, LANES), F32),
                   jax.ShapeDtypeStruct(cnt0.shape, F32)],
        compiler_params=_cparams("arbitrary"),
        name="router",
    )(x, g, wr, cnt0)


def _row_copy(src_ref, src_row, dst_ref, dst_row, sem):
    return pltpu.make_async_copy(src_ref.at[pl.ds(src_row, 1)], dst_ref.at[pl.ds(dst_row, 1)], sem)


def _dispatch_kernel(dest_ref, x_ref, hs_in_ref, hs_ref, sem):
    del hs_in_ref
    tm = x_ref.shape[0]

    def issue(r, _):
        for c in range(2):
            _row_copy(x_ref, r, hs_ref, dest_ref[2 * r + c], sem).start()
        return 0

    def drain(r, _):
        for c in range(2):
            _row_copy(x_ref, 0, hs_ref, 0, sem).wait()
        return 0

    lax.fori_loop(0, tm, issue, 0, unroll=8)
    lax.fori_loop(0, tm, drain, 0, unroll=8)


def _dispatch(x, dest, hs):
    T, D = x.shape
    tm = _tile(T, 512)
    return pl.pallas_call(
        _dispatch_kernel,
        grid=(T // tm,),
        in_specs=[pl.BlockSpec((2 * tm,), lambda i: (i,), memory_space=pltpu.SMEM), _rows(tm, D),
                  pl.BlockSpec(memory_space=pl.ANY)],
        out_specs=pl.BlockSpec(memory_space=pl.ANY),
        out_shape=jax.ShapeDtypeStruct(hs.shape, hs.dtype),
        scratch_shapes=[pltpu.SemaphoreType.DMA(())],
        input_output_aliases={2: 0},
        compiler_params=_cparams("arbitrary"),
        name="moe_dispatch",
    )(dest, x, hs)


def _moe_kernel(te_ref, nu_ref, x_ref, g_ref, *refs, nchunks):
    w_refs, (o_ref, h_scr, acc) = refs[:6], refs[6:]
    i = pl.program_id(0)
    j = pl.program_id(1)

    @pl.when(j == 0)
    def _():
        h_scr[...] = _rms(x_ref[...], g_ref[...]).astype(BF16)
        acc[...] = jnp.zeros_like(acc)

    @pl.when(i < nu_ref[0])
    def _():
        _swiglu_step(j, nchunks, h_scr, w_refs, acc)

    @pl.when(j == pl.num_programs(1) - 1)
    def _():
        o_ref[...] = acc[...]


def _moe_ffn(hs, g, tile_expert, n_used, w_gu, w_down, layer, sub, *, tm):
    Tp, D = hs.shape
    dff = w_down.shape[2]
    nchunks = dff // FF_CHUNK
    grid_spec = pltpu.PrefetchScalarGridSpec(
        num_scalar_prefetch=2,
        grid=(Tp // tm, (nchunks + 1) // 2),
        in_specs=[pl.BlockSpec((tm, D), lambda i, j, te, nu: (i, 0)), _layer(g, layer)]
        + _swiglu_specs(nchunks, D, lambda i, j, te, nu: (sub, te[i]), (None, None)),
        out_specs=pl.BlockSpec((tm, D), lambda i, j, te, nu: (i, 0)),
        scratch_shapes=[pltpu.VMEM((tm, D), BF16), pltpu.VMEM((tm, D), F32)],
    )
    return pl.pallas_call(
        functools.partial(_moe_kernel, nchunks=nchunks),
        grid_spec=grid_spec,
        out_shape=jax.ShapeDtypeStruct((Tp, D), F32),
        compiler_params=_cparams("arbitrary", "arbitrary"),
        name="moe_ffn",
    )(tile_expert, n_used, hs, g, w_gu, w_gu, w_down, w_gu, w_gu, w_down)


def _combine_kernel(dest_ref, x_ref, wt_ref, y_ref, o_ref, buf1, buf2, sem):
    tm = x_ref.shape[0]

    def issue(r, _):
        _row_copy(y_ref, dest_ref[2 * r], buf1, r, sem).start()
        _row_copy(y_ref, dest_ref[2 * r + 1], buf2, r, sem).start()
        return 0

    def drain(r, _):
        for c in range(2):
            _row_copy(y_ref, 0, buf1, 0, sem).wait()
        return 0

    lax.fori_loop(0, tm, issue, 0, unroll=8)
    lax.fori_loop(0, tm, drain, 0, unroll=8)
    wt = wt_ref[...]
    o_ref[...] = x_ref[...] + (wt[:, 0:1] * buf1[...] + wt[:, 1:2] * buf2[...])


def _combine(x, wts, dest, y):
    T, D = x.shape
    tm = _tile(T, 512)
    return pl.pallas_call(
        _combine_kernel,
        grid=(T // tm,),
        in_specs=[pl.BlockSpec((2 * tm,), lambda i: (i,), memory_space=pltpu.SMEM), _rows(tm, D), _rows(tm, LANES),
                  pl.BlockSpec(memory_space=pl.ANY)],
        out_specs=_rows(tm, D),
        out_shape=jax.ShapeDtypeStruct((T, D), F32),
        scratch_shapes=[pltpu.VMEM((tm, D), F32), pltpu.VMEM((tm, D), F32), pltpu.SemaphoreType.DMA(())],
        compiler_params=_cparams("arbitrary"),
        name="moe_combine",
    )(dest, x, wts, y)


def _moe(xs, g, wr, w_gu, w_down, layer, sub):
    sizes = [x.shape[0] for x in xs]
    T = sum(sizes)
    D = xs[0].shape[1]
    tm = _tile(T, 1024)
    cnt = jnp.zeros((SUBLANES, LANES), F32)
    routed = []
    for x in xs:
        idx, wts, cnt = _router(x, g, wr, cnt, layer)
        routed.append((idx, wts))

    counts = cnt[0, :N_EXPERTS].astype(jnp.int32)
    padded = ((counts + tm - 1) // tm) * tm
    ends = jnp.cumsum(padded)
    starts = ends - padded
    n_tiles = (2 * T) // tm + N_EXPERTS
    Tp = n_tiles * tm
    tile_start = jnp.arange(n_tiles, dtype=jnp.int32) * tm
    tile_expert = jnp.minimum(jnp.sum((tile_start[:, None] >= ends[None, :]).astype(jnp.int32), axis=1),
                              N_EXPERTS - 1).astype(jnp.int32)
    n_used = (ends[-1] // tm).astype(jnp.int32).reshape(1)
    experts = jnp.arange(N_EXPERTS, dtype=jnp.int32)

    dests = []
    hs = jnp.zeros((Tp, D), F32)
    for x, (idx, _) in zip(xs, routed):
        e = idx[:, 0:2]
        dest = jnp.sum(jnp.where(e[:, :, None] == experts, starts, 0), axis=2) + idx[:, 2:4]
        dest = dest.reshape(-1)
        dests.append(dest)
        hs = _dispatch(x, dest, hs)
    y = _moe_ffn(hs, g, tile_expert, n_used, w_gu, w_down, layer, sub, tm=tm)
    return [_combine(x, wts, dest, y) for x, (_, wts), dest in zip(xs, routed, dests)]


def _norm_kernel(x_ref, g_ref, o_ref):
    o_ref[...] = _rms(x_ref[...], g_ref[...])


def _norm(x, g):
    T, D = x.shape
    tm = _tile(T, 1024)
    return pl.pallas_call(
        _norm_kernel,
        grid=(T // tm,),
        in_specs=[_rows(tm, D), _full(g.shape)],
        out_specs=_rows(tm, D),
        out_shape=jax.ShapeDtypeStruct((T, D), F32),
        compiler_params=_cparams("arbitrary"),
        name="final_norm",
    )(x, g)


def _s5_block_weights(coef_re, coef_im, b_re, b_im, c_re, c_im):
    depth, G, P, C = b_re.shape
    gpb = S5_CH // C
    nblk = G // gpb
    sst = gpb * P
    bbar_re = coef_re[..., None] * b_re - coef_im[..., None] * b_im
    bbar_im = coef_re[..., None] * b_im + coef_im[..., None] * b_re
    eye = jnp.eye(gpb, dtype=F32)

    def bblk(t):
        t = t.reshape(depth, nblk, gpb, P, C)
        return jnp.einsum('dnkpc,kj->dnkcjp', t, eye).reshape(depth, nblk, S5_CH, sst)

    def cblk(t):
        t = t.reshape(depth, nblk, gpb, C, P)
        return jnp.einsum('dnkcp,kj->dnkpjc', t, eye).reshape(depth, nblk, sst, S5_CH)

    bq = jnp.concatenate([bblk(bbar_re), bblk(bbar_im)], axis=3).astype(BF16)
    cq = jnp.concatenate([cblk(c_re), cblk(-c_im)], axis=2).astype(BF16)
    return bq, cq


def kernel(x_prompt, x_sample, mem_prompt, cache_mem_k, cache_mem_v, state_s5_re, state_s5_im, state_gla,
           norm_mix, w_in, s5_lambda_re, s5_lambda_im, s5_b_re, s5_b_im, s5_c_re, s5_c_im, s5_d, s5_log_dt,
           s5_w_glu, s5_b_glu, s5_norm, gla_w_gate2, gla_b_gate, gla_norm, w_out,
           norm_xa, norm_mem, xa_wq, xa_wk, xa_wv, xa_wo,
           norm_ffn, ffn_w_gu, ffn_w_down, moe_router, moe_w_gu, moe_w_down, norm_final):
    Bp, Lp, D = x_prompt.shape
    Bs, Ls, _ = x_sample.shape
    depth = w_in.shape[0]
    M = mem_prompt.shape[1]
    G, P = s5_lambda_re.shape[1:]
    ns = G * P
    s5w = G * S5_GROUP
    kw = GLA_HEADS * GLA_DK
    vw = GLA_HEADS * GLA_DV

    xp = x_prompt.reshape(Bp * Lp, D)
    xs = x_sample.reshape(Bs * Ls, D)
    mem = mem_prompt.reshape(Bp * M, D)
    vec = lambda v: v.reshape(depth, 1, -1)
    norm_mix, norm_xa, norm_mem, norm_ffn = vec(norm_mix), vec(norm_xa), vec(norm_mem), vec(norm_ffn)
    s5_b_glu, s5_norm, gla_b_gate, gla_norm, s5_d = (vec(s5_b_glu), vec(s5_norm), vec(gla_b_gate), vec(gla_norm),
                                                     vec(s5_d))
    zeros_s5 = jnp.zeros((Bp, ns), F32)
    zeros_gla = jnp.zeros((Bp, GLA_HEADS, GLA_DK, GLA_DV), F32)

    ar, ai, cfr, cfi = _s5_disc(s5_lambda_re.reshape(depth * G, P), s5_lambda_im.reshape(depth * G, P),
                                s5_log_dt.reshape(depth * G, 1))
    bq, cq = _s5_block_weights(cfr.reshape(depth, G, P), cfi.reshape(depth, G, P), s5_b_re, s5_b_im, s5_c_re, s5_c_im)
    ar = ar.reshape(depth, 1, ns)
    ai = ai.reshape(depth, 1, ns)

    outs = {k: [] for k in ('mk', 'mv', 'pr', 'pi', 'pg', 'sr', 'si', 'sg')}
    for i in range(depth):
        mk, mv = _norm_mm(mem, norm_mem, [xa_wk, xa_wv], i)
        mk = mk.reshape(Bp, M, D)
        mv = mv.reshape(Bp, M, D)
        outs['mk'].append(mk.reshape(Bp, M, XA_HEADS, D // XA_HEADS))
        outs['mv'].append(mv.reshape(Bp, M, XA_HEADS, D // XA_HEADS))

        new_x = []
        for grp, x, B, L in (('p', xp, Bp, Lp), ('s', xs, Bs, Ls)):
            prompt = grp == 'p'
            u, qkv, r, la = _mix_in(x, norm_mix, w_in, gla_w_gate2, gla_b_gate, i, su=s5w, sqkv=2 * kw + vw, sr=vw,
                                    time_major_batch=B if prompt else 0)
            if prompt:
                u_tm = u.reshape(L * B, s5w)
                h0r, h0i, s0 = zeros_s5, zeros_s5, zeros_gla
            else:
                u_tm = u.reshape(B, L, s5w).transpose(1, 0, 2).reshape(L * B, s5w)
                h0r = state_s5_re[i].reshape(B, ns)
                h0i = state_s5_im[i].reshape(B, ns)
                s0 = state_gla
            y5, hr, hi = _s5(u_tm, h0r, h0i, ar, ai, bq, cq, s5_d, s5_w_glu, s5_b_glu, s5_norm, i, L=L, B=B)
            if prompt:
                y5 = y5.reshape(L, B * s5w)
            else:
                y5 = y5.reshape(L, B, s5w).transpose(1, 0, 2).reshape(B * L, s5w)
            yg, sg = _gla(qkv, r, la, s0, gla_norm, i, B=B, L=L)
            outs[grp + 'r'].append(hr.reshape(B, G, P))
            outs[grp + 'i'].append(hi.reshape(B, G, P))
            outs[grp + 'g'].append(sg)
            x = _mix_out(x, y5, yg, w_out, i, time_major_batch=B if prompt else 0)
            (q,) = _norm_mm(x, norm_xa, [xa_wq], i)
            if prompt:
                o = _xattn(q, mk, mv, B=B, L=L)
            else:
                o = _xattn_cache(q, cache_mem_k, cache_mem_v, i, B=B, L=L)
            x = _mm_res(x, o, xa_wo, i)
            new_x.append(x)
        xp, xs = new_x
        if i % 2 == 0:
            xp = _ffn(xp, norm_ffn, ffn_w_gu, ffn_w_down, i, i // 2)
            xs = _ffn(xs, norm_ffn, ffn_w_gu, ffn_w_down, i, i // 2)
        else:
            wr = jnp.pad(moe_router[i // 2], ((0, 0), (0, LANES - N_EXPERTS)))
            xp, xs = _moe([xp, xs], norm_ffn, wr, moe_w_gu, moe_w_down, i, i // 2)

    y_prompt = _norm(xp, norm_final.reshape(1, D)).reshape(Bp, Lp, D)
    y_sample = _norm(xs, norm_final.reshape(1, D)).reshape(Bs, Ls, D)
    st = jnp.stack
    return (y_prompt, y_sample, st(outs['mk']), st(outs['mv']), st(outs['pr']), st(outs['pi']), st(outs['pg']),
            st(outs['sr']), st(outs['si']), st(outs['sg']))
```

```python
import functools
import math

import jax
import jax.numpy as jnp
from jax import lax
from jax.experimental import pallas as pl
from jax.experimental.pallas import tpu as pltpu

F32 = jnp.float32
BF16 = jnp.bfloat16
EPS = 1e-6

S5_GROUP = 16
GLA_HEADS = 4
GLA_DK = 64
GLA_DV = 128
GLA_GATE_TAU = 16.0
XA_HEADS = 4
N_EXPERTS = 8
LANES = 128
SUBLANES = 8
VMEM_LIMIT = 56 * 1024 * 1024


def _cparams(*sem):
    return pltpu.CompilerParams(dimension_semantics=sem, vmem_limit_bytes=VMEM_LIMIT)


def _tile(n, pref):
    t = min(n, pref)
    while n % t:
        t //= 2
    return t


def _rms(x, g):
    return x * lax.rsqrt(jnp.mean(x * x, axis=-1, keepdims=True) + EPS) * g


def _mm(a, b):
    return jnp.dot(a, b, preferred_element_type=F32)


def _mm_nt(a, b):
    return lax.dot_general(a, b, (((1,), (1,)), ((), ())), preferred_element_type=F32)


def _mm_tn(a, b):
    return lax.dot_general(a, b, (((0,), (0,)), ((), ())), preferred_element_type=F32)


def _sigmoid(x):
    return 1.0 / (1.0 + jnp.exp(-x))


def _silu(x):
    return x * _sigmoid(x)


def _full(shape):
    return pl.BlockSpec(shape, lambda *_: (0,) * len(shape))


def _rows(tm, width):
    return pl.BlockSpec((tm, width), lambda i, *_: (i, 0))


def _layer(arr, layer):
    nd = arr.ndim - 1
    return pl.BlockSpec((None,) + arr.shape[1:], lambda *_: (layer,) + (0,) * nd, pipeline_mode=pl.Buffered(1))


def _first_step(*axes):
    cond = pl.program_id(axes[0]) == 0
    for a in axes[1:]:
        cond = cond & (pl.program_id(a) == 0)
    return cond


def _mix_in_kernel(x_ref, g_ref, w_ref, wg2_ref, bg_ref, u_ref, qkv_ref, r_ref, la_ref,
                   wu_s, wqkv_s, wgl_s, wr_s, wg2_s, *, su, sqkv, rank, tmajor):
    @pl.when(_first_step(0, 1))
    def _():
        wu_s[...] = w_ref[:, 0:su].astype(BF16)
        wqkv_s[...] = w_ref[:, su:su + sqkv].astype(BF16)
        wgl_s[...] = w_ref[:, su + sqkv:su + sqkv + LANES].astype(BF16)
        wr_s[...] = w_ref[:, su + sqkv + rank:].astype(BF16)
        wg2_s[...] = jnp.zeros_like(wg2_s)
        wg2_s[0:rank, :] = wg2_ref[...].astype(BF16)

    nb, tq, D = x_ref.shape
    rows = nb * tq
    h = _rms(x_ref[...].reshape(rows, D), g_ref[...]).astype(BF16)
    u = _mm(h, wu_s[...])
    if tmajor:
        for b in range(nb):
            for c in range(su // LANES):
                u_ref[c, pl.ds(b, tq, stride=nb), :] = u[b * tq:(b + 1) * tq, c * LANES:(c + 1) * LANES]
    else:
        u_ref[...] = u.reshape(nb, tq, su)
    qkv_ref[...] = _mm(h, wqkv_s[...]).reshape(nb, tq, sqkv)
    r_ref[...] = _mm(h, wr_s[...]).reshape(nb, tq, r_ref.shape[2])
    glr = _mm(h, wgl_s[...])
    pre = _mm(glr.astype(BF16), wg2_s[...]) + bg_ref[...]
    logsig = jnp.minimum(pre, 0.0) - jnp.log(1.0 + jnp.exp(-jnp.abs(pre)))
    la_ref[...] = (logsig * (1.0 / GLA_GATE_TAU)).reshape(nb, tq, la_ref.shape[2])


def _seq_tiles(B, L, tmajor):
    if tmajor:
        return B, _tile(L, max(SUBLANES, 512 // B))
    tq = _tile(L, 512)
    return _tile(B, max(1, 512 // tq)), tq


def _mix_in(x, g, w_in, wg2, bg, layer, *, su, sqkv, sr, tmajor):
    B, L, D = x.shape
    rank = wg2.shape[1]
    kw = wg2.shape[2]
    nb, tq = _seq_tiles(B, L, tmajor)
    blk = lambda w: pl.BlockSpec((nb, tq, w), lambda bi, ti: (bi, ti, 0))
    if tmajor:
        u_shape = jax.ShapeDtypeStruct((su // LANES, L * B, LANES), F32)
        u_spec = pl.BlockSpec((su // LANES, tq * B, LANES), lambda bi, ti: (0, ti, 0))
    else:
        u_shape = jax.ShapeDtypeStruct((B, L, su), F32)
        u_spec = blk(su)
    kern = functools.partial(_mix_in_kernel, su=su, sqkv=sqkv, rank=rank, tmajor=tmajor)
    return pl.pallas_call(
        kern,
        grid=(B // nb, L // tq),
        in_specs=[blk(D), _layer(g, layer), _layer(w_in, layer), _layer(wg2, layer), _layer(bg, layer)],
        out_specs=[u_spec, blk(sqkv), blk(sr), blk(kw)],
        out_shape=[u_shape, jax.ShapeDtypeStruct((B, L, sqkv), F32), jax.ShapeDtypeStruct((B, L, sr), F32),
                   jax.ShapeDtypeStruct((B, L, kw), F32)],
        scratch_shapes=[pltpu.VMEM((D, su), BF16), pltpu.VMEM((D, sqkv), BF16), pltpu.VMEM((D, LANES), BF16),
                        pltpu.VMEM((D, sr), BF16), pltpu.VMEM((LANES, kw), BF16)],
        compiler_params=_cparams("arbitrary", "arbitrary"),
        name="mix_in",
    )(x, g, w_in, wg2, bg)


def _s5_disc_kernel(lr_ref, li_ref, ldt_ref, ar_ref, ai_ref, cr_ref, ci_ref):
    lam_re = jnp.minimum(lr_ref[...], -1e-4)
    lam_im = li_ref[...]
    dt = jnp.exp(ldt_ref[...])
    mag = jnp.exp(lam_re * dt)
    ang = lam_im * dt
    abar_re = mag * jnp.cos(ang)
    abar_im = mag * jnp.sin(ang)
    den = lam_re * lam_re + lam_im * lam_im
    n_re = abar_re - 1.0
    n_im = abar_im
    ar_ref[...] = abar_re
    ai_ref[...] = abar_im
    cr_ref[...] = (n_re * lam_re + n_im * lam_im) / den
    ci_ref[...] = (n_im * lam_re - n_re * lam_im) / den


def _s5_disc(lam_re, lam_im, log_dt):
    shp = lam_re.shape
    out = jax.ShapeDtypeStruct(shp, F32)
    return pl.pallas_call(
        _s5_disc_kernel,
        in_specs=[_full(shp), _full(shp), _full(log_dt.shape)],
        out_specs=[_full(shp)] * 4,
        out_shape=[out] * 4,
        name="s5_disc",
    )(lam_re, lam_im, log_dt)


S5_CH = 128
S5_ROWS_MM = 256


def _s5_kernel(u_ref, h0r_ref, h0i_ref, ar_ref, ai_ref, bq_ref, cq_ref, d_ref, wglu_ref, bglu_ref, gn_ref,
               y_ref, hr_out, hi_out, hbuf, ybuf, wglu_s, *, tt, bb):
    j = pl.program_id(0)
    ns = hr_out.shape[1]
    nblk = bq_ref.shape[0]
    sst = ns // nblk
    rows = tt * bb
    rmm = min(S5_ROWS_MM, rows)

    @pl.when(j == 0)
    def _():
        hr_out[...] = h0r_ref[...]
        hi_out[...] = h0i_ref[...]
        wglu_s[...] = wglu_ref[...].astype(BF16)

    for n in range(nblk):
        cs = slice(n * sst, (n + 1) * sst)
        cis = slice(ns + n * sst, ns + (n + 1) * sst)
        chs = slice(n * S5_CH, (n + 1) * S5_CH)
        for r in range(rows // rmm):
            rs = slice(r * rmm, (r + 1) * rmm)
            bu = _mm(u_ref[n, rs, :].astype(BF16), bq_ref[n])
            hbuf[rs, cs] = bu[:, :sst]
            hbuf[rs, cis] = bu[:, sst:]

        def row_block(rb, _):
            r0 = pl.multiple_of(rb * SUBLANES, SUBLANES)
            ar = jnp.broadcast_to(ar_ref[:, cs], (SUBLANES, sst))
            ai = jnp.broadcast_to(ai_ref[:, cs], (SUBLANES, sst))

            def step(t, carry):
                hr, hi = carry
                rr = pl.multiple_of(t * bb + r0, SUBLANES)
                nr = ar * hr - ai * hi + hbuf[pl.ds(rr, SUBLANES), cs]
                ni = ar * hi + ai * hr + hbuf[pl.ds(rr, SUBLANES), cis]
                hbuf[pl.ds(rr, SUBLANES), cs] = nr
                hbuf[pl.ds(rr, SUBLANES), cis] = ni
                return nr, ni

            hr, hi = lax.fori_loop(0, tt, step,
                                   (hr_out[pl.ds(r0, SUBLANES), cs], hi_out[pl.ds(r0, SUBLANES), cs]),
                                   unroll=min(tt, 8))
            hr_out[pl.ds(r0, SUBLANES), cs] = hr
            hi_out[pl.ds(r0, SUBLANES), cs] = hi
            return 0

        lax.fori_loop(0, bb // SUBLANES, row_block, 0)

        for r in range(rows // rmm):
            rs = slice(r * rmm, (r + 1) * rmm)
            ybuf[rs, chs] = (_mm(hbuf[rs, cs].astype(BF16), cq_ref[n, 0:sst, :])
                             + _mm(hbuf[rs, cis].astype(BF16), cq_ref[n, sst:2 * sst, :])
                             + d_ref[:, chs] * u_ref[n, rs, :])

    for r in range(rows // rmm):
        rs = slice(r * rmm, (r + 1) * rmm)
        y = ybuf[rs, :]
        z = 0.5 * y * (1.0 + jnp.tanh(math.sqrt(2.0 / math.pi) * (y + 0.044715 * (y * y * y))))
        z = z * _sigmoid(_mm(z.astype(BF16), wglu_s[...]) + bglu_ref[...])
        out = _rms(z, gn_ref[...])
        for n in range(nblk):
            y_ref[n, rs, :] = out[:, n * S5_CH:(n + 1) * S5_CH]


def _s5(u, h0r, h0i, ar, ai, bq, cq, d, wglu, bglu, gn, layer, *, L, B):
    nblk, _, cw = u.shape
    W = nblk * cw
    ns = h0r.shape[1]
    tt = _tile(L, max(8, 512 // B))
    kern = functools.partial(_s5_kernel, tt=tt, bb=B)
    uspec = pl.BlockSpec((nblk, tt * B, cw), lambda i: (0, i, 0))
    return pl.pallas_call(
        kern,
        grid=(L // tt,),
        in_specs=[uspec, _full((B, ns)), _full((B, ns)),
                  _layer(ar, layer), _layer(ai, layer), _layer(bq, layer), _layer(cq, layer), _layer(d, layer),
                  _layer(wglu, layer), _layer(bglu, layer), _layer(gn, layer)],
        out_specs=[uspec, _full((B, ns)), _full((B, ns))],
        out_shape=[jax.ShapeDtypeStruct(u.shape, F32), jax.ShapeDtypeStruct((B, ns), F32),
                   jax.ShapeDtypeStruct((B, ns), F32)],
        scratch_shapes=[pltpu.VMEM((tt * B, 2 * ns), F32), pltpu.VMEM((tt * B, W), F32),
                        pltpu.VMEM(wglu.shape[1:], BF16)],
        compiler_params=_cparams("arbitrary"),
        name="s5",
    )(u, h0r, h0i, ar, ai, bq, cq, d, wglu, bglu, gn)


def _gla_kernel(qkv_ref, r_ref, la_ref, s0_ref, gn_ref, y_ref, s_ref, *, nseq, ls):
    c = pl.program_id(1)
    rows = nseq * ls
    kw = GLA_HEADS * GLA_DK

    @pl.when(c == 0)
    def _():
        s_ref[...] = s0_ref[...]

    ri = lax.broadcasted_iota(jnp.int32, (rows, rows), 0)
    ci = lax.broadcasted_iota(jnp.int32, (rows, rows), 1)
    causal = ri >= ci
    if nseq > 1:
        causal = causal & ((ri // ls) == (ci // ls))
    tril = jnp.where(causal, 1.0, 0.0).astype(BF16)

    la = la_ref[...].reshape(rows, kw)
    p0 = la.astype(BF16)
    r1 = la - p0.astype(F32)
    p1 = r1.astype(BF16)
    p2 = (r1 - p1.astype(F32)).astype(BF16)
    bcum = _mm(tril, p0) + _mm(tril, p1) + _mm(tril, p2)

    qkv = qkv_ref[...].reshape(rows, qkv_ref.shape[2])
    rg = r_ref[...].reshape(rows, r_ref.shape[2])
    q = qkv[:, 0:kw] * (GLA_DK ** -0.5)
    k = qkv[:, kw:2 * kw]
    qd = (q * jnp.exp(bcum)).astype(BF16)
    kd = (k * jnp.exp(-bcum)).astype(BF16)
    eye = (lax.broadcasted_iota(jnp.int32, (GLA_DK, GLA_DK), 0)
           == lax.broadcasted_iota(jnp.int32, (GLA_DK, GLA_DK), 1))

    for h in range(GLA_HEADS):
        hs = slice(h * GLA_DK, (h + 1) * GLA_DK)
        vs = slice(2 * kw + h * GLA_DV, 2 * kw + (h + 1) * GLA_DV)
        v = qkv[:, vs].astype(BF16)
        att = jnp.where(causal, _mm_nt(qd[:, hs], kd[:, hs]), 0.0)
        o_intra = _mm(att.astype(BF16), v)
        o_parts = []
        for s in range(nseq):
            rs = slice(s * ls, (s + 1) * ls)
            state = s_ref[s, h]
            o_parts.append(o_intra[rs] + _mm(qd[rs, hs], state.astype(BF16)))
            blast = bcum[(s + 1) * ls - 1:(s + 1) * ls, hs]
            k2 = (k[rs, hs] * jnp.exp(blast - bcum[rs, hs])).astype(BF16)
            dec_col = jnp.sum(jnp.where(eye, jnp.exp(blast), 0.0), axis=1, keepdims=True)
            s_ref[s, h] = dec_col * state + _mm_tn(k2, v[rs])
        o = o_parts[0] if nseq == 1 else jnp.concatenate(o_parts, axis=0)
        y = _rms(o, gn_ref[...]) * _silu(rg[:, h * GLA_DV:(h + 1) * GLA_DV])
        y_ref[:, :, h * GLA_DV:(h + 1) * GLA_DV] = y.reshape(nseq, ls, GLA_DV)


def _gla(qkv, r, la, s0, gn, layer):
    B, L, _ = qkv.shape
    nseq = _tile(B, 8)
    ls = _tile(L, 64)
    nc = L // ls
    kern = functools.partial(_gla_kernel, nseq=nseq, ls=ls)
    sshape = (B, GLA_HEADS, GLA_DK, GLA_DV)
    sblk = (nseq,) + sshape[1:]
    blk3 = lambda a: pl.BlockSpec((nseq, ls, a.shape[2]), lambda b, c: (b, c, 0))
    vw = GLA_HEADS * GLA_DV
    if s0.ndim == 5:
        s0_spec = pl.BlockSpec((None,) + sblk, lambda b, c: (layer, b, 0, 0, 0))
    else:
        s0_spec = pl.BlockSpec(sblk, lambda b, c: (b, 0, 0, 0))
    y, s = pl.pallas_call(
        kern,
        grid=(B // nseq, nc),
        in_specs=[blk3(qkv), blk3(r), blk3(la), s0_spec, _layer(gn, layer)],
        out_specs=[pl.BlockSpec((nseq, ls, vw), lambda b, c: (b, c, 0)),
                   pl.BlockSpec(sblk, lambda b, c: (b, 0, 0, 0))],
        out_shape=[jax.ShapeDtypeStruct((B, L, vw), F32), jax.ShapeDtypeStruct(sshape, F32)],
        compiler_params=_cparams("arbitrary", "arbitrary"),
        name="gla",
    )(qkv, r, la, s0, gn)
    return y, s


def _mix_out_kernel(x_ref, a_ref, b_ref, w_ref, gq_ref, wq_ref, o_ref, q_ref, w_s, wq_s, *, tmajor):
    @pl.when(_first_step(0, 1))
    def _():
        w_s[...] = w_ref[...].astype(BF16)
        wq_s[...] = wq_ref[...].astype(BF16)

    nb, tq, D = x_ref.shape
    rows = nb * tq
    if tmajor:
        a = jnp.concatenate(
            [jnp.concatenate([a_ref[c, pl.ds(b, tq, stride=nb), :] for c in range(a_ref.shape[0])], axis=1)
             for b in range(nb)], axis=0)
    else:
        a = a_ref[...].reshape(rows, a_ref.shape[2])
    wa = a.shape[1]
    b = b_ref[...].reshape(rows, b_ref.shape[2])
    x1 = (x_ref[...].reshape(rows, D) + _mm(a.astype(BF16), w_s[0:wa, :]) + _mm(b.astype(BF16), w_s[wa:, :]))
    o_ref[...] = x1.reshape(nb, tq, D)
    q_ref[...] = _mm(_rms(x1, gq_ref[...]).astype(BF16), wq_s[...]).reshape(nb, tq, q_ref.shape[2])


def _mix_out(x, ys5, ygla, w_out, gq, wq, layer, *, tmajor):
    B, L, D = x.shape
    nb, tq = _seq_tiles(B, L, tmajor)
    blk = lambda w: pl.BlockSpec((nb, tq, w), lambda bi, ti: (bi, ti, 0))
    if tmajor:
        a_spec = pl.BlockSpec((ys5.shape[0], tq * B, ys5.shape[2]), lambda bi, ti: (0, ti, 0))
    else:
        a_spec = blk(ys5.shape[2])
    dq = wq.shape[2]
    return pl.pallas_call(
        functools.partial(_mix_out_kernel, tmajor=tmajor),
        grid=(B // nb, L // tq),
        in_specs=[blk(D), a_spec, blk(ygla.shape[2]), _layer(w_out, layer), _layer(gq, layer), _layer(wq, layer)],
        out_specs=[blk(D), blk(dq)],
        out_shape=[jax.ShapeDtypeStruct((B, L, D), F32), jax.ShapeDtypeStruct((B, L, dq), F32)],
        scratch_shapes=[pltpu.VMEM(w_out.shape[1:], BF16), pltpu.VMEM(wq.shape[1:], BF16)],
        compiler_params=_cparams("arbitrary", "arbitrary"),
        name="mix_out",
    )(x, ys5, ygla, w_out, gq, wq)


def _norm_mm_kernel(*refs, nw):
    x_ref, g_ref = refs[0], refs[1]
    w_refs = refs[2:2 + nw]
    o_refs = refs[2 + nw:2 + 2 * nw]
    w_ss = refs[2 + 2 * nw:]

    @pl.when(_first_step(0))
    def _():
        for w_ref, w_s in zip(w_refs, w_ss):
            w_s[...] = w_ref[...].astype(BF16)

    h = _rms(x_ref[...], g_ref[...]).astype(BF16)
    for o_ref, w_s in zip(o_refs, w_ss):
        o_ref[...] = _mm(h, w_s[...])


def _norm_mm(x, g, ws, layer):
    T, D = x.shape
    tm = _tile(T, 512)
    nw = len(ws)
    outs = pl.pallas_call(
        functools.partial(_norm_mm_kernel, nw=nw),
        grid=(T // tm,),
        in_specs=[_rows(tm, D), _layer(g, layer)] + [_layer(w, layer) for w in ws],
        out_specs=[_rows(tm, w.shape[2]) for w in ws],
        out_shape=[jax.ShapeDtypeStruct((T, w.shape[2]), F32) for w in ws],
        scratch_shapes=[pltpu.VMEM(w.shape[1:], BF16) for w in ws],
        compiler_params=_cparams("arbitrary"),
        name="norm_mm",
    )(x, g, *ws)
    return outs


def _mm_res_kernel(x_ref, a_ref, w_ref, o_ref, w_s):
    @pl.when(_first_step(0))
    def _():
        w_s[...] = w_ref[...].astype(BF16)

    o_ref[...] = x_ref[...] + _mm(a_ref[...].astype(BF16), w_s[...])


def _mm_res(x, a, w, layer):
    T, D = x.shape
    tm = _tile(T, 512)
    return pl.pallas_call(
        _mm_res_kernel,
        grid=(T // tm,),
        in_specs=[_rows(tm, D), _rows(tm, a.shape[1]), _layer(w, layer)],
        out_specs=_rows(tm, D),
        out_shape=jax.ShapeDtypeStruct((T, D), F32),
        scratch_shapes=[pltpu.VMEM(w.shape[1:], BF16)],
        compiler_params=_cparams("arbitrary"),
        name="mm_res",
    )(x, a, w)


def _xattn_kernel(q_ref, k_ref, v_ref, o_ref):
    dh = q_ref.shape[1] // XA_HEADS
    scale = dh ** -0.5
    for h in range(XA_HEADS):
        hs = slice(h * dh, (h + 1) * dh)
        q = q_ref[:, hs].astype(BF16)
        k = k_ref[0, :, hs].astype(BF16)
        v = v_ref[0, :, hs].astype(BF16)
        sc = _mm_nt(q, k) * scale
        p = jnp.exp(sc - jnp.max(sc, axis=-1, keepdims=True))
        p = p / jnp.sum(p, axis=-1, keepdims=True)
        o_ref[:, hs] = _mm(p.astype(BF16), v)


def _xattn(q, mk, mv, *, B, L):
    T, D = q.shape
    M = mk.shape[1]
    lq = _tile(L, 512)
    nc = L // lq
    return pl.pallas_call(
        _xattn_kernel,
        grid=(B, nc),
        in_specs=[pl.BlockSpec((lq, D), lambda b, c: (b * nc + c, 0)),
                  pl.BlockSpec((1, M, D), lambda b, c: (b, 0, 0)),
                  pl.BlockSpec((1, M, D), lambda b, c: (b, 0, 0))],
        out_specs=pl.BlockSpec((lq, D), lambda b, c: (b * nc + c, 0)),
        out_shape=jax.ShapeDtypeStruct((T, D), F32),
        compiler_params=_cparams("arbitrary", "arbitrary"),
        name="xattn",
    )(q, mk, mv)


def _xattn_cache_kernel(q_ref, k_ref, v_ref, o_ref, *, nseq, lq, m):
    D = q_ref.shape[1]
    dh = D // XA_HEADS
    nc = dh // LANES
    hc = XA_HEADS * nc
    scale = dh ** -0.5
    npad = LANES - XA_HEADS * lq
    for s in range(nseq):
        rs = slice(s * lq, (s + 1) * lq)
        sts = []
        for h in range(XA_HEADS):
            acc = None
            for c in range(nc):
                k = k_ref[s, pl.ds(c * XA_HEADS + h, m, stride=hc), :].astype(BF16)
                qs = q_ref[rs, h * dh + c * LANES:h * dh + (c + 1) * LANES].astype(BF16)
                t = _mm_nt(k, qs)
                acc = t if acc is None else acc + t
            sts.append(acc)
        st = jnp.concatenate(sts + [jnp.zeros((m, npad), F32)], axis=1) * scale
        e = jnp.exp(st - jnp.max(st, axis=0, keepdims=True))
        p = (e / jnp.sum(e, axis=0, keepdims=True)).T
        for h in range(XA_HEADS):
            ph = p[h * lq:(h + 1) * lq, :].astype(BF16)
            for c in range(nc):
                v = v_ref[s, pl.ds(c * XA_HEADS + h, m, stride=hc), :].astype(BF16)
                o_ref[rs, h * dh + c * LANES:h * dh + (c + 1) * LANES] = _mm(ph, v)


def _xattn_cache(q, ck, cv, layer, *, B, L):
    T, D = q.shape
    depth, _, M, H, dh = ck.shape
    nc = dh // LANES

    def view(c):
        c = c.reshape(depth, B, M, H, nc, LANES).transpose(0, 1, 2, 4, 3, 5)
        return c.reshape(depth, B, M * nc * H, LANES)

    nseq = _tile(B, 4)
    kern = functools.partial(_xattn_cache_kernel, nseq=nseq, lq=L, m=M)
    cspec = pl.BlockSpec((None, nseq, M * nc * H, LANES), lambda b: (layer, b, 0, 0))
    return pl.pallas_call(
        kern,
        grid=(B // nseq,),
        in_specs=[_rows(nseq * L, D), cspec, cspec],
        out_specs=_rows(nseq * L, D),
        out_shape=jax.ShapeDtypeStruct((T, D), F32),
        compiler_params=_cparams("arbitrary"),
        name="xattn_cache",
    )(q, view(ck), view(cv))


FF_CHUNK = 256


def _swiglu_chunk(h, wg_ref, wu_ref, wd_ref):
    act = _silu(_mm(h, wg_ref[...].astype(BF16))) * _mm(h, wu_ref[...].astype(BF16))
    return _mm(act.astype(BF16), wd_ref[...].astype(BF16))


def _swiglu_step(j, nchunks, h_scr, w_refs, acc):
    wga, wua, wda, wgb, wub, wdb = w_refs

    @pl.when(2 * j + 1 < nchunks)
    def _():
        h = h_scr[...]
        acc[...] += _swiglu_chunk(h, wga, wua, wda) + _swiglu_chunk(h, wgb, wub, wdb)

    @pl.when(2 * j + 1 >= nchunks)
    def _():
        acc[...] += _swiglu_chunk(h_scr[...], wga, wua, wda)


def _swiglu_specs(nchunks, D, lead, lead_none, step=lambda i, j, *p: j):
    second = lambda j: jnp.minimum(2 * j + 1, nchunks - 1)
    specs = []
    for blk in (lambda j: 2 * j, second):
        col = lambda i, j, *p, blk=blk: blk(step(i, j, *p))
        specs += [
            pl.BlockSpec(lead_none + (D, FF_CHUNK), lambda i, j, *p, col=col: lead(i, j, *p) + (0, col(i, j, *p))),
            pl.BlockSpec(lead_none + (D, FF_CHUNK),
                         lambda i, j, *p, col=col: lead(i, j, *p) + (0, nchunks + col(i, j, *p))),
            pl.BlockSpec(lead_none + (FF_CHUNK, D), lambda i, j, *p, col=col: lead(i, j, *p) + (col(i, j, *p), 0))]
    return specs


def _ffn_kernel(x_ref, g_ref, *refs, nchunks):
    w_refs, (o_ref, h_scr, acc) = refs[:6], refs[6:]
    j = pl.program_id(1)

    @pl.when(j == 0)
    def _():
        h_scr[...] = _rms(x_ref[...], g_ref[...]).astype(BF16)
        acc[...] = jnp.zeros_like(acc)

    _swiglu_step(j, nchunks, h_scr, w_refs, acc)

    @pl.when(j == pl.num_programs(1) - 1)
    def _():
        o_ref[...] = x_ref[...] + acc[...]


def _ffn(x, g, w_gu, w_down, layer, sub):
    T, D = x.shape
    dff = w_down.shape[1]
    tm = _tile(T, 1024)
    nchunks = dff // FF_CHUNK
    return pl.pallas_call(
        functools.partial(_ffn_kernel, nchunks=nchunks),
        grid=(T // tm, (nchunks + 1) // 2),
        in_specs=[pl.BlockSpec((tm, D), lambda i, j: (i, 0)), _layer(g, layer)]
        + _swiglu_specs(nchunks, D, lambda i, j: (sub,), (None,)),
        out_specs=pl.BlockSpec((tm, D), lambda i, j: (i, 0)),
        out_shape=jax.ShapeDtypeStruct((T, D), F32),
        scratch_shapes=[pltpu.VMEM((tm, D), BF16), pltpu.VMEM((tm, D), F32)],
        compiler_params=_cparams("arbitrary", "arbitrary"),
        name="ffn",
    )(x, g, w_gu, w_gu, w_down, w_gu, w_gu, w_down)


def _router_kernel(x_ref, g_ref, wr_ref, cnt0_ref, idx_ref, wt_ref, cnt_ref):
    @pl.when(_first_step(0))
    def _():
        cnt_ref[...] = cnt0_ref[...]

    tm = x_ref.shape[0]
    h = _rms(x_ref[...], g_ref[...])
    logits = jnp.dot(h, wr_ref[...], preferred_element_type=F32, precision=lax.Precision.HIGHEST)
    lane = lax.broadcasted_iota(jnp.int32, logits.shape, 1)
    neg = jnp.float32(-jnp.inf)
    logits = jnp.where(lane < N_EXPERTS, logits, neg)
    v1 = jnp.max(logits, axis=-1, keepdims=True)
    i1 = jnp.min(jnp.where(logits == v1, lane, LANES), axis=-1, keepdims=True)
    rest = jnp.where(lane == i1, neg, logits)
    v2 = jnp.max(rest, axis=-1, keepdims=True)
    i2 = jnp.min(jnp.where(rest == v2, lane, LANES), axis=-1, keepdims=True)
    e2 = jnp.exp(v2 - v1)
    w1 = 1.0 / (1.0 + e2)
    w2 = e2 / (1.0 + e2)
    ri = lax.broadcasted_iota(jnp.int32, (tm, tm), 0)
    ci = lax.broadcasted_iota(jnp.int32, (tm, tm), 1)
    tril = jnp.where(ri >= ci, 1.0, 0.0).astype(BF16)
    cum1 = _mm(tril, jnp.where(lane == i1, 1.0, 0.0).astype(BF16))
    cum2 = _mm(tril, jnp.where(lane == i2, 1.0, 0.0).astype(BF16))
    base = cnt_ref[0:1, :]
    tot1 = cum1[tm - 1:tm, :]
    tot2 = cum2[tm - 1:tm, :]
    r1 = jnp.sum(jnp.where(lane == i1, base + cum1, 0.0), axis=-1, keepdims=True) - 1.0
    r2 = jnp.sum(jnp.where(lane == i2, base + tot1 + cum2, 0.0), axis=-1, keepdims=True) - 1.0
    cnt_ref[...] = jnp.broadcast_to(base + tot1 + tot2, cnt_ref.shape)
    idx_ref[...] = jnp.where(lane == 0, i1, jnp.where(lane == 1, i2, jnp.where(
        lane == 2, r1.astype(jnp.int32), jnp.where(lane == 3, r2.astype(jnp.int32), 0))))
    wt_ref[...] = jnp.where(lane == 0, w1, jnp.where(lane == 1, w2, 0.0))


def _router(x, g, wr, cnt0, layer):
    T, D = x.shape
    tm = _tile(T, 512)
    return pl.pallas_call(
        _router_kernel,
        grid=(T // tm,),
        in_specs=[_rows(tm, D), _layer(g, layer), _full(wr.shape), _full(cnt0.shape)],
        out_specs=[_rows(tm, LANES), _rows(tm, LANES), _full(cnt0.shape)],
        out_shape=[jax.ShapeDtypeStruct((T, LANES), jnp.int32), jax.ShapeDtypeStruct((T, LANES), F32),
                   jax.ShapeDtypeStruct(cnt0.shape, F32)],
        compiler_params=_cparams("arbitrary"),
        name="router",
    )(x, g, wr, cnt0)


def _row_copy(src_ref, src_row, dst_ref, dst_row, sem):
    return pltpu.make_async_copy(src_ref.at[pl.ds(src_row, 1)], dst_ref.at[pl.ds(dst_row, 1)], sem)


def _dispatch_kernel(dest_ref, x_ref, hs_in_ref, hs_ref, sem):
    del hs_in_ref
    tm = x_ref.shape[0]

    def issue(r, _):
        for c in range(2):
            _row_copy(x_ref, r, hs_ref, dest_ref[2 * r + c], sem).start()
        return 0

    def drain(r, _):
        for c in range(2):
            _row_copy(x_ref, 0, hs_ref, 0, sem).wait()
        return 0

    lax.fori_loop(0, tm, issue, 0, unroll=8)
    lax.fori_loop(0, tm, drain, 0, unroll=8)


def _dispatch(x, dest, hs):
    T, D = x.shape
    tm = _tile(T, 512)
    return pl.pallas_call(
        _dispatch_kernel,
        grid=(T // tm,),
        in_specs=[pl.BlockSpec((2 * tm,), lambda i: (i,), memory_space=pltpu.SMEM), _rows(tm, D),
                  pl.BlockSpec(memory_space=pl.ANY)],
        out_specs=pl.BlockSpec(memory_space=pl.ANY),
        out_shape=jax.ShapeDtypeStruct(hs.shape, hs.dtype),
        scratch_shapes=[pltpu.SemaphoreType.DMA(())],
        input_output_aliases={2: 0},
        compiler_params=_cparams("arbitrary"),
        name="moe_dispatch",
    )(dest, x, hs)


def _moe_kernel(te_ref, nu_ref, x_ref, g_ref, *refs, nchunks):
    w_refs, (o_ref, h_scr, acc) = refs[:6], refs[6:]
    i = pl.program_id(0)
    j = pl.program_id(1)

    @pl.when(j == 0)
    def _():
        h_scr[...] = _rms(x_ref[...], g_ref[...]).astype(BF16)
        acc[...] = jnp.zeros_like(acc)

    @pl.when(i < nu_ref[0])
    def _():
        _swiglu_step(j, nchunks, h_scr, w_refs, acc)

    @pl.when(j == pl.num_programs(1) - 1)
    def _():
        o_ref[...] = acc[...]


def _moe_ffn(hs, g, tile_expert, n_used, w_gu, w_down, layer, sub, *, tm):
    Tp, D = hs.shape
    dff = w_down.shape[2]
    nchunks = dff // FF_CHUNK
    nsteps = (nchunks + 1) // 2
    step = lambda i, j, te, nu: jnp.where(i < nu[0], j, nsteps - 1)
    grid_spec = pltpu.PrefetchScalarGridSpec(
        num_scalar_prefetch=2,
        grid=(Tp // tm, nsteps),
        in_specs=[pl.BlockSpec((tm, D), lambda i, j, te, nu: (i, 0)), _layer(g, layer)]
        + _swiglu_specs(nchunks, D, lambda i, j, te, nu: (sub, te[i]), (None, None), step),
        out_specs=pl.BlockSpec((tm, D), lambda i, j, te, nu: (i, 0)),
        scratch_shapes=[pltpu.VMEM((tm, D), BF16), pltpu.VMEM((tm, D), F32)],
    )
    return pl.pallas_call(
        functools.partial(_moe_kernel, nchunks=nchunks),
        grid_spec=grid_spec,
        out_shape=jax.ShapeDtypeStruct((Tp, D), F32),
        compiler_params=_cparams("arbitrary", "arbitrary"),
        name="moe_ffn",
    )(tile_expert, n_used, hs, g, w_gu, w_gu, w_down, w_gu, w_gu, w_down)


def _combine_kernel(dest_ref, x_ref, wt_ref, y_ref, o_ref, buf1, buf2, sem):
    tm = x_ref.shape[0]

    def issue(r, _):
        _row_copy(y_ref, dest_ref[2 * r], buf1, r, sem).start()
        _row_copy(y_ref, dest_ref[2 * r + 1], buf2, r, sem).start()
        return 0

    def drain(r, _):
        for c in range(2):
            _row_copy(y_ref, 0, buf1, 0, sem).wait()
        return 0

    lax.fori_loop(0, tm, issue, 0, unroll=8)
    lax.fori_loop(0, tm, drain, 0, unroll=8)
    wt = wt_ref[...]
    o_ref[...] = x_ref[...] + (wt[:, 0:1] * buf1[...] + wt[:, 1:2] * buf2[...])


def _combine(x, wts, dest, y):
    T, D = x.shape
    tm = _tile(T, 512)
    return pl.pallas_call(
        _combine_kernel,
        grid=(T // tm,),
        in_specs=[pl.BlockSpec((2 * tm,), lambda i: (i,), memory_space=pltpu.SMEM), _rows(tm, D), _rows(tm, LANES),
                  pl.BlockSpec(memory_space=pl.ANY)],
        out_specs=_rows(tm, D),
        out_shape=jax.ShapeDtypeStruct((T, D), F32),
        scratch_shapes=[pltpu.VMEM((tm, D), F32), pltpu.VMEM((tm, D), F32), pltpu.SemaphoreType.DMA(())],
        compiler_params=_cparams("arbitrary"),
        name="moe_combine",
    )(dest, x, wts, y)


def _moe(xs, g, wr, w_gu, w_down, layer, sub):
    sizes = [x.shape[0] for x in xs]
    T = sum(sizes)
    D = xs[0].shape[1]
    tm = _tile(T, 1024)
    cnt = jnp.zeros((SUBLANES, LANES), F32)
    routed = []
    for x in xs:
        idx, wts, cnt = _router(x, g, wr, cnt, layer)
        routed.append((idx, wts))

    counts = cnt[0, :N_EXPERTS].astype(jnp.int32)
    padded = ((counts + tm - 1) // tm) * tm
    ends = jnp.cumsum(padded)
    starts = ends - padded
    n_tiles = (2 * T) // tm + N_EXPERTS
    Tp = n_tiles * tm
    tile_start = jnp.arange(n_tiles, dtype=jnp.int32) * tm
    tile_expert = jnp.sum((tile_start[:, None] >= ends[None, :]).astype(jnp.int32), axis=1)
    n_used = (ends[-1] // tm).astype(jnp.int32).reshape(1)
    tile_idx = jnp.arange(n_tiles, dtype=jnp.int32)
    last_expert = jnp.sum(jnp.where(tile_idx == n_used - 1, tile_expert, 0))
    tile_expert = jnp.where(tile_idx < n_used, tile_expert, last_expert).astype(jnp.int32)
    experts = jnp.arange(N_EXPERTS, dtype=jnp.int32)

    dests = []
    hs = jnp.zeros((Tp, D), F32)
    for x, (idx, _) in zip(xs, routed):
        e = idx[:, 0:2]
        dest = jnp.sum(jnp.where(e[:, :, None] == experts, starts, 0), axis=2) + idx[:, 2:4]
        dest = dest.reshape(-1)
        dests.append(dest)
        hs = _dispatch(x, dest, hs)
    y = _moe_ffn(hs, g, tile_expert, n_used, w_gu, w_down, layer, sub, tm=tm)
    return [_combine(x, wts, dest, y) for x, (_, wts), dest in zip(xs, routed, dests)]


def _norm_kernel(x_ref, g_ref, o_ref):
    o_ref[...] = _rms(x_ref[...], g_ref[...])


def _norm(x, g):
    T, D = x.shape
    tm = _tile(T, 1024)
    return pl.pallas_call(
        _norm_kernel,
        grid=(T // tm,),
        in_specs=[_rows(tm, D), _full(g.shape)],
        out_specs=_rows(tm, D),
        out_shape=jax.ShapeDtypeStruct((T, D), F32),
        compiler_params=_cparams("arbitrary"),
        name="final_norm",
    )(x, g)


def _s5_block_weights(coef_re, coef_im, b_re, b_im, c_re, c_im):
    depth, G, P, C = b_re.shape
    gpb = S5_CH // C
    nblk = G // gpb
    sst = gpb * P
    bbar_re = coef_re[..., None] * b_re - coef_im[..., None] * b_im
    bbar_im = coef_re[..., None] * b_im + coef_im[..., None] * b_re
    eye = jnp.eye(gpb, dtype=F32)

    def bblk(t):
        t = t.reshape(depth, nblk, gpb, P, C)
        return jnp.einsum('dnkpc,kj->dnkcjp', t, eye).reshape(depth, nblk, S5_CH, sst)

    def cblk(t):
        t = t.reshape(depth, nblk, gpb, C, P)
        return jnp.einsum('dnkcp,kj->dnkpjc', t, eye).reshape(depth, nblk, sst, S5_CH)

    bq = jnp.concatenate([bblk(bbar_re), bblk(bbar_im)], axis=3).astype(BF16)
    cq = jnp.concatenate([cblk(c_re), cblk(-c_im)], axis=2).astype(BF16)
    return bq, cq


def kernel(x_prompt, x_sample, mem_prompt, cache_mem_k, cache_mem_v, state_s5_re, state_s5_im, state_gla,
           norm_mix, w_in, s5_lambda_re, s5_lambda_im, s5_b_re, s5_b_im, s5_c_re, s5_c_im, s5_d, s5_log_dt,
           s5_w_glu, s5_b_glu, s5_norm, gla_w_gate2, gla_b_gate, gla_norm, w_out,
           norm_xa, norm_mem, xa_wq, xa_wk, xa_wv, xa_wo,
           norm_ffn, ffn_w_gu, ffn_w_down, moe_router, moe_w_gu, moe_w_down, norm_final):
    Bp, Lp, D = x_prompt.shape
    Bs, Ls, _ = x_sample.shape
    depth = w_in.shape[0]
    M = mem_prompt.shape[1]
    G, P = s5_lambda_re.shape[1:]
    ns = G * P
    s5w = G * S5_GROUP
    kw = GLA_HEADS * GLA_DK
    vw = GLA_HEADS * GLA_DV

    xp = x_prompt.reshape(Bp * Lp, D)
    xs = x_sample.reshape(Bs * Ls, D)
    mem = mem_prompt.reshape(Bp * M, D)
    vec = lambda v: v.reshape(depth, 1, -1)
    norm_mix, norm_xa, norm_mem, norm_ffn = vec(norm_mix), vec(norm_xa), vec(norm_mem), vec(norm_ffn)
    s5_b_glu, s5_norm, gla_b_gate, gla_norm, s5_d = (vec(s5_b_glu), vec(s5_norm), vec(gla_b_gate), vec(gla_norm),
                                                     vec(s5_d))
    zeros_s5 = jnp.zeros((Bp, ns), F32)
    zeros_gla = jnp.zeros((Bp, GLA_HEADS, GLA_DK, GLA_DV), F32)

    ar, ai, cfr, cfi = _s5_disc(s5_lambda_re.reshape(depth * G, P), s5_lambda_im.reshape(depth * G, P),
                                s5_log_dt.reshape(depth * G, 1))
    bq, cq = _s5_block_weights(cfr.reshape(depth, G, P), cfi.reshape(depth, G, P), s5_b_re, s5_b_im, s5_c_re, s5_c_im)
    ar = ar.reshape(depth, 1, ns)
    ai = ai.reshape(depth, 1, ns)

    outs = {k: [] for k in ('mk', 'mv', 'pr', 'pi', 'pg', 'sr', 'si', 'sg')}
    for i in range(depth):
        mk, mv = _norm_mm(mem, norm_mem, [xa_wk, xa_wv], i)
        mk = mk.reshape(Bp, M, D)
        mv = mv.reshape(Bp, M, D)
        outs['mk'].append(mk.reshape(Bp, M, XA_HEADS, D // XA_HEADS))
        outs['mv'].append(mv.reshape(Bp, M, XA_HEADS, D // XA_HEADS))

        new_x = []
        for grp, x, B, L in (('p', xp, Bp, Lp), ('s', xs, Bs, Ls)):
            prompt = grp == 'p'
            u, qkv, r, la = _mix_in(x.reshape(B, L, D), norm_mix, w_in, gla_w_gate2, gla_b_gate, i,
                                    su=s5w, sqkv=2 * kw + vw, sr=vw, tmajor=prompt)
            if prompt:
                u_tm = u
                h0r, h0i, s0 = zeros_s5, zeros_s5, zeros_gla
            else:
                u_tm = u.reshape(B, L, s5w // LANES, LANES).transpose(2, 1, 0, 3).reshape(s5w // LANES, L * B, LANES)
                h0r = state_s5_re[i].reshape(B, ns)
                h0i = state_s5_im[i].reshape(B, ns)
                s0 = state_gla
            y5, hr, hi = _s5(u_tm, h0r, h0i, ar, ai, bq, cq, s5_d, s5_w_glu, s5_b_glu, s5_norm, i, L=L, B=B)
            if not prompt:
                y5 = y5.reshape(s5w // LANES, L, B, LANES).transpose(2, 1, 0, 3).reshape(B, L, s5w)
            yg, sg = _gla(qkv, r, la, s0, gla_norm, i)
            outs[grp + 'r'].append(hr.reshape(B, G, P))
            outs[grp + 'i'].append(hi.reshape(B, G, P))
            outs[grp + 'g'].append(sg)
            x, q = _mix_out(x.reshape(B, L, D), y5, yg, w_out, norm_xa, xa_wq, i, tmajor=prompt)
            x = x.reshape(B * L, D)
            q = q.reshape(B * L, D)
            if prompt:
                o = _xattn(q, mk, mv, B=B, L=L)
            else:
                o = _xattn_cache(q, cache_mem_k, cache_mem_v, i, B=B, L=L)
            x = _mm_res(x, o, xa_wo, i)
            new_x.append(x)
        xp, xs = new_x
        if i % 2 == 0:
            xp = _ffn(xp, norm_ffn, ffn_w_gu, ffn_w_down, i, i // 2)
            xs = _ffn(xs, norm_ffn, ffn_w_gu, ffn_w_down, i, i // 2)
        else:
            wr = jnp.pad(moe_router[i // 2], ((0, 0), (0, LANES - N_EXPERTS)))
            xp, xs = _moe([xp, xs], norm_ffn, wr, moe_w_gu, moe_w_down, i, i // 2)

    y_prompt = _norm(xp, norm_final.reshape(1, D)).reshape(Bp, Lp, D)
    y_sample = _norm(xs, norm_final.reshape(1, D)).reshape(Bs, Ls, D)
    st = jnp.stack
    return (y_prompt, y_sample, st(outs['mk']), st(outs['mv']), st(outs['pr']), st(outs['pi']), st(outs['pg']),
            st(outs['sr']), st(outs['si']), st(outs['sg']))
```

```python
import functools
import math

import jax
import jax.numpy as jnp
from jax import lax
from jax.experimental import pallas as pl
from jax.experimental.pallas import tpu as pltpu

F32 = jnp.float32
BF16 = jnp.bfloat16
EPS = 1e-6

S5_GROUP = 16
GLA_HEADS = 4
GLA_DK = 64
GLA_DV = 128
GLA_GATE_TAU = 16.0
XA_HEADS = 4
N_EXPERTS = 8
LANES = 128
SUBLANES = 8
VMEM_LIMIT = 56 * 1024 * 1024


def _cparams(*sem):
    return pltpu.CompilerParams(dimension_semantics=sem, vmem_limit_bytes=VMEM_LIMIT)


def _tile(n, pref):
    t = min(n, pref)
    while n % t:
        t //= 2
    return t


def _rms(x, g):
    return x * lax.rsqrt(jnp.mean(x * x, axis=-1, keepdims=True) + EPS) * g


def _mm(a, b):
    return jnp.dot(a, b, preferred_element_type=F32)


def _mm_nt(a, b):
    return lax.dot_general(a, b, (((1,), (1,)), ((), ())), preferred_element_type=F32)


def _mm_tn(a, b):
    return lax.dot_general(a, b, (((0,), (0,)), ((), ())), preferred_element_type=F32)


def _sigmoid(x):
    return 1.0 / (1.0 + jnp.exp(-x))


def _silu(x):
    return x * _sigmoid(x)


def _full(shape):
    return pl.BlockSpec(shape, lambda *_: (0,) * len(shape))


def _rows(tm, width):
    return pl.BlockSpec((tm, width), lambda i, *_: (i, 0))


def _layer(arr, layer):
    nd = arr.ndim - 1
    return pl.BlockSpec((None,) + arr.shape[1:], lambda *_: (layer,) + (0,) * nd, pipeline_mode=pl.Buffered(1))


def _first_step(*axes):
    cond = pl.program_id(axes[0]) == 0
    for a in axes[1:]:
        cond = cond & (pl.program_id(a) == 0)
    return cond


def _mix_in_kernel(x_ref, g_ref, w_ref, wg2_ref, bg_ref, u_ref, qkv_ref, r_ref, la_ref,
                   wu_s, wqkv_s, wgl_s, wr_s, wg2_s, *, su, sqkv, rank, tmajor):
    @pl.when(_first_step(0, 1))
    def _():
        wu_s[...] = w_ref[:, 0:su].astype(BF16)
        wqkv_s[...] = w_ref[:, su:su + sqkv].astype(BF16)
        wgl_s[...] = w_ref[:, su + sqkv:su + sqkv + LANES].astype(BF16)
        wr_s[...] = w_ref[:, su + sqkv + rank:].astype(BF16)
        wg2_s[...] = jnp.zeros_like(wg2_s)
        wg2_s[0:rank, :] = wg2_ref[...].astype(BF16)

    nb, tq, D = x_ref.shape
    rows = nb * tq
    h = _rms(x_ref[...].reshape(rows, D), g_ref[...]).astype(BF16)
    u = _mm(h, wu_s[...])
    if tmajor:
        for b in range(nb):
            for c in range(su // LANES):
                u_ref[c, pl.ds(b, tq, stride=nb), :] = u[b * tq:(b + 1) * tq, c * LANES:(c + 1) * LANES]
    else:
        u_ref[...] = u.reshape(nb, tq, su)
    qkv_ref[...] = _mm(h, wqkv_s[...]).reshape(nb, tq, sqkv)
    r_ref[...] = _mm(h, wr_s[...]).reshape(nb, tq, r_ref.shape[2])
    glr = _mm(h, wgl_s[...])
    pre = _mm(glr.astype(BF16), wg2_s[...]) + bg_ref[...]
    logsig = jnp.minimum(pre, 0.0) - jnp.log(1.0 + jnp.exp(-jnp.abs(pre)))
    la_ref[...] = (logsig * (1.0 / GLA_GATE_TAU)).reshape(nb, tq, la_ref.shape[2])


def _seq_tiles(B, L, tmajor):
    if tmajor:
        return B, _tile(L, max(SUBLANES, 512 // B))
    tq = _tile(L, 512)
    return _tile(B, max(1, 512 // tq)), tq


def _mix_in(x, g, w_in, wg2, bg, layer, *, su, sqkv, sr, tmajor):
    B, L, D = x.shape
    rank = wg2.shape[1]
    kw = wg2.shape[2]
    nb, tq = _seq_tiles(B, L, tmajor)
    blk = lambda w: pl.BlockSpec((nb, tq, w), lambda bi, ti: (bi, ti, 0))
    if tmajor:
        u_shape = jax.ShapeDtypeStruct((su // LANES, L * B, LANES), F32)
        u_spec = pl.BlockSpec((su // LANES, tq * B, LANES), lambda bi, ti: (0, ti, 0))
    else:
        u_shape = jax.ShapeDtypeStruct((B, L, su), F32)
        u_spec = blk(su)
    kern = functools.partial(_mix_in_kernel, su=su, sqkv=sqkv, rank=rank, tmajor=tmajor)
    return pl.pallas_call(
        kern,
        grid=(B // nb, L // tq),
        in_specs=[blk(D), _layer(g, layer), _layer(w_in, layer), _layer(wg2, layer), _layer(bg, layer)],
        out_specs=[u_spec, blk(sqkv), blk(sr), blk(kw)],
        out_shape=[u_shape, jax.ShapeDtypeStruct((B, L, sqkv), F32), jax.ShapeDtypeStruct((B, L, sr), F32),
                   jax.ShapeDtypeStruct((B, L, kw), F32)],
        scratch_shapes=[pltpu.VMEM((D, su), BF16), pltpu.VMEM((D, sqkv), BF16), pltpu.VMEM((D, LANES), BF16),
                        pltpu.VMEM((D, sr), BF16), pltpu.VMEM((LANES, kw), BF16)],
        compiler_params=_cparams("arbitrary", "arbitrary"),
        name="mix_in",
    )(x, g, w_in, wg2, bg)


def _s5_disc_kernel(lr_ref, li_ref, ldt_ref, ar_ref, ai_ref, cr_ref, ci_ref):
    lam_re = jnp.minimum(lr_ref[...], -1e-4)
    lam_im = li_ref[...]
    dt = jnp.exp(ldt_ref[...])
    mag = jnp.exp(lam_re * dt)
    ang = lam_im * dt
    abar_re = mag * jnp.cos(ang)
    abar_im = mag * jnp.sin(ang)
    den = lam_re * lam_re + lam_im * lam_im
    n_re = abar_re - 1.0
    n_im = abar_im
    ar_ref[...] = abar_re
    ai_ref[...] = abar_im
    cr_ref[...] = (n_re * lam_re + n_im * lam_im) / den
    ci_ref[...] = (n_im * lam_re - n_re * lam_im) / den


def _s5_disc(lam_re, lam_im, log_dt):
    shp = lam_re.shape
    out = jax.ShapeDtypeStruct(shp, F32)
    return pl.pallas_call(
        _s5_disc_kernel,
        in_specs=[_full(shp), _full(shp), _full(log_dt.shape)],
        out_specs=[_full(shp)] * 4,
        out_shape=[out] * 4,
        name="s5_disc",
    )(lam_re, lam_im, log_dt)


S5_CH = 128
S5_ROWS_MM = 256


def _s5_kernel(u_ref, h0r_ref, h0i_ref, ar_ref, ai_ref, bq_ref, cq_ref, d_ref, wglu_ref, bglu_ref, gn_ref,
               y_ref, hr_out, hi_out, hbuf, ybuf, wglu_s, *, tt, bb):
    j = pl.program_id(0)
    ns = hr_out.shape[1]
    nblk = bq_ref.shape[0]
    sst = ns // nblk
    rows = tt * bb
    rmm = min(S5_ROWS_MM, rows)

    @pl.when(j == 0)
    def _():
        hr_out[...] = h0r_ref[...]
        hi_out[...] = h0i_ref[...]
        wglu_s[...] = wglu_ref[...].astype(BF16)

    for n in range(nblk):
        cs = slice(n * sst, (n + 1) * sst)
        cis = slice(ns + n * sst, ns + (n + 1) * sst)
        chs = slice(n * S5_CH, (n + 1) * S5_CH)
        for r in range(rows // rmm):
            rs = slice(r * rmm, (r + 1) * rmm)
            bu = _mm(u_ref[n, rs, :].astype(BF16), bq_ref[n])
            hbuf[rs, cs] = bu[:, :sst]
            hbuf[rs, cis] = bu[:, sst:]

        def one_row_block():
            ar = jnp.broadcast_to(ar_ref[:, cs], (SUBLANES, sst))
            ai = jnp.broadcast_to(ai_ref[:, cs], (SUBLANES, sst))
            hr = hr_out[:, cs]
            hi = hi_out[:, cs]
            for t in range(tt):
                rr = slice(t * SUBLANES, (t + 1) * SUBLANES)
                hr, hi = (ar * hr - ai * hi + hbuf[rr, cs], ar * hi + ai * hr + hbuf[rr, cis])
                hbuf[rr, cs] = hr
                hbuf[rr, cis] = hi
            hr_out[:, cs] = hr
            hi_out[:, cs] = hi

        def row_block(rb, _):
            r0 = pl.multiple_of(rb * SUBLANES, SUBLANES)
            ar = jnp.broadcast_to(ar_ref[:, cs], (SUBLANES, sst))
            ai = jnp.broadcast_to(ai_ref[:, cs], (SUBLANES, sst))

            def step(t, carry):
                hr, hi = carry
                rr = pl.multiple_of(t * bb + r0, SUBLANES)
                nr = ar * hr - ai * hi + hbuf[pl.ds(rr, SUBLANES), cs]
                ni = ar * hi + ai * hr + hbuf[pl.ds(rr, SUBLANES), cis]
                hbuf[pl.ds(rr, SUBLANES), cs] = nr
                hbuf[pl.ds(rr, SUBLANES), cis] = ni
                return nr, ni

            hr, hi = lax.fori_loop(0, tt, step,
                                   (hr_out[pl.ds(r0, SUBLANES), cs], hi_out[pl.ds(r0, SUBLANES), cs]),
                                   unroll=min(tt, 8))
            hr_out[pl.ds(r0, SUBLANES), cs] = hr
            hi_out[pl.ds(r0, SUBLANES), cs] = hi
            return 0

        if bb == SUBLANES:
            one_row_block()
        else:
            lax.fori_loop(0, bb // SUBLANES, row_block, 0)

        for r in range(rows // rmm):
            rs = slice(r * rmm, (r + 1) * rmm)
            ybuf[rs, chs] = (_mm(hbuf[rs, cs].astype(BF16), cq_ref[n, 0:sst, :])
                             + _mm(hbuf[rs, cis].astype(BF16), cq_ref[n, sst:2 * sst, :])
                             + d_ref[:, chs] * u_ref[n, rs, :])

    for r in range(rows // rmm):
        rs = slice(r * rmm, (r + 1) * rmm)
        y = ybuf[rs, :]
        z = 0.5 * y * (1.0 + jnp.tanh(math.sqrt(2.0 / math.pi) * (y + 0.044715 * (y * y * y))))
        z = z * _sigmoid(_mm(z.astype(BF16), wglu_s[...]) + bglu_ref[...])
        out = _rms(z, gn_ref[...])
        for n in range(nblk):
            y_ref[n, rs, :] = out[:, n * S5_CH:(n + 1) * S5_CH]


def _s5(u, h0r, h0i, ar, ai, bq, cq, d, wglu, bglu, gn, layer, *, L, B):
    nblk, _, cw = u.shape
    W = nblk * cw
    ns = h0r.shape[1]
    tt = _tile(L, max(8, 512 // B))
    kern = functools.partial(_s5_kernel, tt=tt, bb=B)
    uspec = pl.BlockSpec((nblk, tt * B, cw), lambda i: (0, i, 0))
    return pl.pallas_call(
        kern,
        grid=(L // tt,),
        in_specs=[uspec, _full((B, ns)), _full((B, ns)),
                  _layer(ar, layer), _layer(ai, layer), _layer(bq, layer), _layer(cq, layer), _layer(d, layer),
                  _layer(wglu, layer), _layer(bglu, layer), _layer(gn, layer)],
        out_specs=[uspec, _full((B, ns)), _full((B, ns))],
        out_shape=[jax.ShapeDtypeStruct(u.shape, F32), jax.ShapeDtypeStruct((B, ns), F32),
                   jax.ShapeDtypeStruct((B, ns), F32)],
        scratch_shapes=[pltpu.VMEM((tt * B, 2 * ns), F32), pltpu.VMEM((tt * B, W), F32),
                        pltpu.VMEM(wglu.shape[1:], BF16)],
        compiler_params=_cparams("arbitrary"),
        name="s5",
    )(u, h0r, h0i, ar, ai, bq, cq, d, wglu, bglu, gn)


def _gla_kernel(qkv_ref, r_ref, la_ref, s0_ref, gn_ref, y_ref, s_ref, *, nseq, ls):
    c = pl.program_id(1)
    rows = nseq * ls
    kw = GLA_HEADS * GLA_DK

    @pl.when(c == 0)
    def _():
        s_ref[...] = s0_ref[...]

    ri = lax.broadcasted_iota(jnp.int32, (rows, rows), 0)
    ci = lax.broadcasted_iota(jnp.int32, (rows, rows), 1)
    causal = ri >= ci
    if nseq > 1:
        causal = causal & ((ri // ls) == (ci // ls))
    tril = jnp.where(causal, 1.0, 0.0).astype(BF16)

    la = la_ref[...].reshape(rows, kw)
    p0 = la.astype(BF16)
    p1 = (la - p0.astype(F32)).astype(BF16)
    bcum = _mm(tril, p0) + _mm(tril, p1)

    qkv = qkv_ref[...].reshape(rows, qkv_ref.shape[2])
    rg = r_ref[...].reshape(rows, r_ref.shape[2])
    q = qkv[:, 0:kw] * (GLA_DK ** -0.5)
    k = qkv[:, kw:2 * kw]
    qd = (q * jnp.exp(bcum)).astype(BF16)
    kd = (k * jnp.exp(-bcum)).astype(BF16)
    eye = (lax.broadcasted_iota(jnp.int32, (GLA_DK, GLA_DK), 0)
           == lax.broadcasted_iota(jnp.int32, (GLA_DK, GLA_DK), 1))

    for h in range(GLA_HEADS):
        hs = slice(h * GLA_DK, (h + 1) * GLA_DK)
        vs = slice(2 * kw + h * GLA_DV, 2 * kw + (h + 1) * GLA_DV)
        v = qkv[:, vs].astype(BF16)
        att = jnp.where(causal, _mm_nt(qd[:, hs], kd[:, hs]), 0.0)
        o_intra = _mm(att.astype(BF16), v)
        o_parts = []
        for s in range(nseq):
            rs = slice(s * ls, (s + 1) * ls)
            state = s_ref[s, h]
            o_parts.append(o_intra[rs] + _mm(qd[rs, hs], state.astype(BF16)))
            blast = bcum[(s + 1) * ls - 1:(s + 1) * ls, hs]
            k2 = (k[rs, hs] * jnp.exp(blast - bcum[rs, hs])).astype(BF16)
            dec_col = jnp.sum(jnp.where(eye, jnp.exp(blast), 0.0), axis=1, keepdims=True)
            s_ref[s, h] = dec_col * state + _mm_tn(k2, v[rs])
        o = o_parts[0] if nseq == 1 else jnp.concatenate(o_parts, axis=0)
        y = _rms(o, gn_ref[...]) * _silu(rg[:, h * GLA_DV:(h + 1) * GLA_DV])
        y_ref[:, :, h * GLA_DV:(h + 1) * GLA_DV] = y.reshape(nseq, ls, GLA_DV)


def _gla(qkv, r, la, s0, gn, layer):
    B, L, _ = qkv.shape
    nseq = _tile(B, 8)
    ls = _tile(L, 64)
    nc = L // ls
    kern = functools.partial(_gla_kernel, nseq=nseq, ls=ls)
    sshape = (B, GLA_HEADS, GLA_DK, GLA_DV)
    sblk = (nseq,) + sshape[1:]
    blk3 = lambda a: pl.BlockSpec((nseq, ls, a.shape[2]), lambda b, c: (b, c, 0))
    vw = GLA_HEADS * GLA_DV
    if s0.ndim == 5:
        s0_spec = pl.BlockSpec((None,) + sblk, lambda b, c: (layer, b, 0, 0, 0))
    else:
        s0_spec = pl.BlockSpec(sblk, lambda b, c: (b, 0, 0, 0))
    y, s = pl.pallas_call(
        kern,
        grid=(B // nseq, nc),
        in_specs=[blk3(qkv), blk3(r), blk3(la), s0_spec, _layer(gn, layer)],
        out_specs=[pl.BlockSpec((nseq, ls, vw), lambda b, c: (b, c, 0)),
                   pl.BlockSpec(sblk, lambda b, c: (b, 0, 0, 0))],
        out_shape=[jax.ShapeDtypeStruct((B, L, vw), F32), jax.ShapeDtypeStruct(sshape, F32)],
        compiler_params=_cparams("arbitrary", "arbitrary"),
        name="gla",
    )(qkv, r, la, s0, gn)
    return y, s


def _mix_out_kernel(x_ref, a_ref, b_ref, w_ref, gq_ref, wq_ref, o_ref, q_ref, w_s, wq_s, *, tmajor):
    @pl.when(_first_step(0, 1))
    def _():
        w_s[...] = w_ref[...].astype(BF16)
        wq_s[...] = wq_ref[...].astype(BF16)

    nb, tq, D = x_ref.shape
    rows = nb * tq
    if tmajor:
        a = jnp.concatenate(
            [jnp.concatenate([a_ref[c, pl.ds(b, tq, stride=nb), :] for c in range(a_ref.shape[0])], axis=1)
             for b in range(nb)], axis=0)
    else:
        a = a_ref[...].reshape(rows, a_ref.shape[2])
    wa = a.shape[1]
    b = b_ref[...].reshape(rows, b_ref.shape[2])
    x1 = (x_ref[...].reshape(rows, D) + _mm(a.astype(BF16), w_s[0:wa, :]) + _mm(b.astype(BF16), w_s[wa:, :]))
    o_ref[...] = x1.reshape(nb, tq, D)
    q_ref[...] = _mm(_rms(x1, gq_ref[...]).astype(BF16), wq_s[...]).reshape(nb, tq, q_ref.shape[2])


def _mix_out(x, ys5, ygla, w_out, gq, wq, layer, *, tmajor):
    B, L, D = x.shape
    nb, tq = _seq_tiles(B, L, tmajor)
    blk = lambda w: pl.BlockSpec((nb, tq, w), lambda bi, ti: (bi, ti, 0))
    if tmajor:
        a_spec = pl.BlockSpec((ys5.shape[0], tq * B, ys5.shape[2]), lambda bi, ti: (0, ti, 0))
    else:
        a_spec = blk(ys5.shape[2])
    dq = wq.shape[2]
    return pl.pallas_call(
        functools.partial(_mix_out_kernel, tmajor=tmajor),
        grid=(B // nb, L // tq),
        in_specs=[blk(D), a_spec, blk(ygla.shape[2]), _layer(w_out, layer), _layer(gq, layer), _layer(wq, layer)],
        out_specs=[blk(D), blk(dq)],
        out_shape=[jax.ShapeDtypeStruct((B, L, D), F32), jax.ShapeDtypeStruct((B, L, dq), F32)],
        scratch_shapes=[pltpu.VMEM(w_out.shape[1:], BF16), pltpu.VMEM(wq.shape[1:], BF16)],
        compiler_params=_cparams("arbitrary", "arbitrary"),
        name="mix_out",
    )(x, ys5, ygla, w_out, gq, wq)


def _norm_mm_kernel(*refs, nw):
    x_ref, g_ref = refs[0], refs[1]
    w_refs = refs[2:2 + nw]
    o_refs = refs[2 + nw:2 + 2 * nw]
    w_ss = refs[2 + 2 * nw:]

    @pl.when(_first_step(0))
    def _():
        for w_ref, w_s in zip(w_refs, w_ss):
            w_s[...] = w_ref[...].astype(BF16)

    h = _rms(x_ref[...], g_ref[...]).astype(BF16)
    for o_ref, w_s in zip(o_refs, w_ss):
        o_ref[...] = _mm(h, w_s[...])


def _norm_mm(x, g, ws, layer):
    T, D = x.shape
    tm = _tile(T, 512)
    nw = len(ws)
    outs = pl.pallas_call(
        functools.partial(_norm_mm_kernel, nw=nw),
        grid=(T // tm,),
        in_specs=[_rows(tm, D), _layer(g, layer)] + [_layer(w, layer) for w in ws],
        out_specs=[_rows(tm, w.shape[2]) for w in ws],
        out_shape=[jax.ShapeDtypeStruct((T, w.shape[2]), F32) for w in ws],
        scratch_shapes=[pltpu.VMEM(w.shape[1:], BF16) for w in ws],
        compiler_params=_cparams("arbitrary"),
        name="norm_mm",
    )(x, g, *ws)
    return outs


def _mm_res_kernel(x_ref, a_ref, w_ref, o_ref, w_s):
    @pl.when(_first_step(0))
    def _():
        w_s[...] = w_ref[...].astype(BF16)

    o_ref[...] = x_ref[...] + _mm(a_ref[...].astype(BF16), w_s[...])


def _mm_res(x, a, w, layer):
    T, D = x.shape
    tm = _tile(T, 512)
    return pl.pallas_call(
        _mm_res_kernel,
        grid=(T // tm,),
        in_specs=[_rows(tm, D), _rows(tm, a.shape[1]), _layer(w, layer)],
        out_specs=_rows(tm, D),
        out_shape=jax.ShapeDtypeStruct((T, D), F32),
        scratch_shapes=[pltpu.VMEM(w.shape[1:], BF16)],
        compiler_params=_cparams("arbitrary"),
        name="mm_res",
    )(x, a, w)


def _xattn_kernel(q_ref, k_ref, v_ref, o_ref):
    dh = q_ref.shape[1] // XA_HEADS
    scale = dh ** -0.5
    for h in range(XA_HEADS):
        hs = slice(h * dh, (h + 1) * dh)
        q = q_ref[:, hs].astype(BF16)
        k = k_ref[0, :, hs].astype(BF16)
        v = v_ref[0, :, hs].astype(BF16)
        sc = _mm_nt(q, k) * scale
        p = jnp.exp(sc - jnp.max(sc, axis=-1, keepdims=True))
        p = p / jnp.sum(p, axis=-1, keepdims=True)
        o_ref[:, hs] = _mm(p.astype(BF16), v)


def _xattn(q, mk, mv, *, B, L):
    T, D = q.shape
    M = mk.shape[1]
    lq = _tile(L, 512)
    nc = L // lq
    return pl.pallas_call(
        _xattn_kernel,
        grid=(B, nc),
        in_specs=[pl.BlockSpec((lq, D), lambda b, c: (b * nc + c, 0)),
                  pl.BlockSpec((1, M, D), lambda b, c: (b, 0, 0)),
                  pl.BlockSpec((1, M, D), lambda b, c: (b, 0, 0))],
        out_specs=pl.BlockSpec((lq, D), lambda b, c: (b * nc + c, 0)),
        out_shape=jax.ShapeDtypeStruct((T, D), F32),
        compiler_params=_cparams("arbitrary", "arbitrary"),
        name="xattn",
    )(q, mk, mv)


def _xattn_cache_kernel(q_ref, k_ref, v_ref, o_ref, *, nseq, lq, m):
    D = q_ref.shape[1]
    dh = D // XA_HEADS
    nc = dh // LANES
    hc = XA_HEADS * nc
    scale = dh ** -0.5
    npad = LANES - XA_HEADS * lq
    for s in range(nseq):
        rs = slice(s * lq, (s + 1) * lq)
        sts = []
        for h in range(XA_HEADS):
            acc = None
            for c in range(nc):
                k = k_ref[s, pl.ds(c * XA_HEADS + h, m, stride=hc), :].astype(BF16)
                qs = q_ref[rs, h * dh + c * LANES:h * dh + (c + 1) * LANES].astype(BF16)
                t = _mm_nt(k, qs)
                acc = t if acc is None else acc + t
            sts.append(acc)
        st = jnp.concatenate(sts + [jnp.zeros((m, npad), F32)], axis=1) * scale
        e = jnp.exp(st - jnp.max(st, axis=0, keepdims=True))
        p = (e / jnp.sum(e, axis=0, keepdims=True)).T
        for h in range(XA_HEADS):
            ph = p[h * lq:(h + 1) * lq, :].astype(BF16)
            for c in range(nc):
                v = v_ref[s, pl.ds(c * XA_HEADS + h, m, stride=hc), :].astype(BF16)
                o_ref[rs, h * dh + c * LANES:h * dh + (c + 1) * LANES] = _mm(ph, v)


def _xattn_cache(q, ck, cv, layer, *, B, L):
    T, D = q.shape
    depth, _, M, H, dh = ck.shape
    nc = dh // LANES

    def view(c):
        c = c.reshape(depth, B, M, H, nc, LANES).transpose(0, 1, 2, 4, 3, 5)
        return c.reshape(depth, B, M * nc * H, LANES)

    nseq = _tile(B, 4)
    kern = functools.partial(_xattn_cache_kernel, nseq=nseq, lq=L, m=M)
    cspec = pl.BlockSpec((None, nseq, M * nc * H, LANES), lambda b: (layer, b, 0, 0))
    return pl.pallas_call(
        kern,
        grid=(B // nseq,),
        in_specs=[_rows(nseq * L, D), cspec, cspec],
        out_specs=_rows(nseq * L, D),
        out_shape=jax.ShapeDtypeStruct((T, D), F32),
        compiler_params=_cparams("arbitrary"),
        name="xattn_cache",
    )(q, view(ck), view(cv))


FF_CHUNK = 256


def _swiglu_chunk(h, wg_ref, wu_ref, wd_ref):
    act = _silu(_mm(h, wg_ref[...].astype(BF16))) * _mm(h, wu_ref[...].astype(BF16))
    return _mm(act.astype(BF16), wd_ref[...].astype(BF16))


def _swiglu_step(j, nchunks, h_scr, w_refs, acc):
    wga, wua, wda, wgb, wub, wdb = w_refs

    @pl.when(2 * j + 1 < nchunks)
    def _():
        h = h_scr[...]
        acc[...] += _swiglu_chunk(h, wga, wua, wda) + _swiglu_chunk(h, wgb, wub, wdb)

    @pl.when(2 * j + 1 >= nchunks)
    def _():
        acc[...] += _swiglu_chunk(h_scr[...], wga, wua, wda)


def _swiglu_specs(nchunks, D, lead, lead_none, step=lambda i, j, *p: j):
    second = lambda j: jnp.minimum(2 * j + 1, nchunks - 1)
    specs = []
    for blk in (lambda j: 2 * j, second):
        col = lambda i, j, *p, blk=blk: blk(step(i, j, *p))
        specs += [
            pl.BlockSpec(lead_none + (D, FF_CHUNK), lambda i, j, *p, col=col: lead(i, j, *p) + (0, col(i, j, *p))),
            pl.BlockSpec(lead_none + (D, FF_CHUNK),
                         lambda i, j, *p, col=col: lead(i, j, *p) + (0, nchunks + col(i, j, *p))),
            pl.BlockSpec(lead_none + (FF_CHUNK, D), lambda i, j, *p, col=col: lead(i, j, *p) + (col(i, j, *p), 0))]
    return specs


def _ffn_kernel(x_ref, g_ref, *refs, nchunks):
    w_refs, (o_ref, h_scr, acc) = refs[:6], refs[6:]
    j = pl.program_id(1)

    @pl.when(j == 0)
    def _():
        h_scr[...] = _rms(x_ref[...], g_ref[...]).astype(BF16)
        acc[...] = jnp.zeros_like(acc)

    _swiglu_step(j, nchunks, h_scr, w_refs, acc)

    @pl.when(j == pl.num_programs(1) - 1)
    def _():
        o_ref[...] = x_ref[...] + acc[...]


def _ffn(x, g, w_gu, w_down, layer, sub):
    T, D = x.shape
    dff = w_down.shape[1]
    tm = _tile(T, 1024)
    nchunks = dff // FF_CHUNK
    return pl.pallas_call(
        functools.partial(_ffn_kernel, nchunks=nchunks),
        grid=(T // tm, (nchunks + 1) // 2),
        in_specs=[pl.BlockSpec((tm, D), lambda i, j: (i, 0)), _layer(g, layer)]
        + _swiglu_specs(nchunks, D, lambda i, j: (sub,), (None,)),
        out_specs=pl.BlockSpec((tm, D), lambda i, j: (i, 0)),
        out_shape=jax.ShapeDtypeStruct((T, D), F32),
        scratch_shapes=[pltpu.VMEM((tm, D), BF16), pltpu.VMEM((tm, D), F32)],
        compiler_params=_cparams("arbitrary", "arbitrary"),
        name="ffn",
    )(x, g, w_gu, w_gu, w_down, w_gu, w_gu, w_down)


def _router_kernel(x_ref, g_ref, wr_ref, cnt0_ref, idx_ref, wt_ref, cnt_ref):
    @pl.when(_first_step(0))
    def _():
        cnt_ref[...] = cnt0_ref[...]

    tm = x_ref.shape[0]
    h = _rms(x_ref[...], g_ref[...])
    logits = jnp.dot(h, wr_ref[...], preferred_element_type=F32, precision=lax.Precision.HIGHEST)
    lane = lax.broadcasted_iota(jnp.int32, logits.shape, 1)
    neg = jnp.float32(-jnp.inf)
    logits = jnp.where(lane < N_EXPERTS, logits, neg)
    v1 = jnp.max(logits, axis=-1, keepdims=True)
    i1 = jnp.min(jnp.where(logits == v1, lane, LANES), axis=-1, keepdims=True)
    rest = jnp.where(lane == i1, neg, logits)
    v2 = jnp.max(rest, axis=-1, keepdims=True)
    i2 = jnp.min(jnp.where(rest == v2, lane, LANES), axis=-1, keepdims=True)
    e2 = jnp.exp(v2 - v1)
    w1 = 1.0 / (1.0 + e2)
    w2 = e2 / (1.0 + e2)
    ri = lax.broadcasted_iota(jnp.int32, (tm, tm), 0)
    ci = lax.broadcasted_iota(jnp.int32, (tm, tm), 1)
    tril = jnp.where(ri >= ci, 1.0, 0.0).astype(BF16)
    cum1 = _mm(tril, jnp.where(lane == i1, 1.0, 0.0).astype(BF16))
    cum2 = _mm(tril, jnp.where(lane == i2, 1.0, 0.0).astype(BF16))
    base = cnt_ref[0:1, :]
    tot1 = cum1[tm - 1:tm, :]
    tot2 = cum2[tm - 1:tm, :]
    r1 = jnp.sum(jnp.where(lane == i1, base + cum1, 0.0), axis=-1, keepdims=True) - 1.0
    r2 = jnp.sum(jnp.where(lane == i2, base + tot1 + cum2, 0.0), axis=-1, keepdims=True) - 1.0
    cnt_ref[...] = jnp.broadcast_to(base + tot1 + tot2, cnt_ref.shape)
    idx_ref[...] = jnp.where(lane == 0, i1, jnp.where(lane == 1, i2, jnp.where(
        lane == 2, r1.astype(jnp.int32), jnp.where(lane == 3, r2.astype(jnp.int32), 0))))
    wt_ref[...] = jnp.where(lane == 0, w1, jnp.where(lane == 1, w2, 0.0))


def _router(x, g, wr, cnt0, layer):
    T, D = x.shape
    tm = _tile(T, 512)
    return pl.pallas_call(
        _router_kernel,
        grid=(T // tm,),
        in_specs=[_rows(tm, D), _layer(g, layer), _full(wr.shape), _full(cnt0.shape)],
        out_specs=[_rows(tm, LANES), _rows(tm, LANES), _full(cnt0.shape)],
        out_shape=[jax.ShapeDtypeStruct((T, LANES), jnp.int32), jax.ShapeDtypeStruct((T, LANES), F32),
                   jax.ShapeDtypeStruct(cnt0.shape, F32)],
        compiler_params=_cparams("arbitrary"),
        name="router",
    )(x, g, wr, cnt0)


def _row_copy(src_ref, src_row, dst_ref, dst_row, sem):
    return pltpu.make_async_copy(src_ref.at[pl.ds(src_row, 1)], dst_ref.at[pl.ds(dst_row, 1)], sem)


def _dispatch_kernel(dest_ref, x_ref, hs_in_ref, hs_ref, sem):
    del hs_in_ref
    tm = x_ref.shape[0]

    def issue(r, _):
        for c in range(2):
            _row_copy(x_ref, r, hs_ref, dest_ref[2 * r + c], sem).start()
        return 0

    def drain(r, _):
        for c in range(2):
            _row_copy(x_ref, 0, hs_ref, 0, sem).wait()
        return 0

    lax.fori_loop(0, tm, issue, 0, unroll=8)
    lax.fori_loop(0, tm, drain, 0, unroll=8)


def _dispatch(x, dest, hs):
    T, D = x.shape
    tm = _tile(T, 512)
    return pl.pallas_call(
        _dispatch_kernel,
        grid=(T // tm,),
        in_specs=[pl.BlockSpec((2 * tm,), lambda i: (i,), memory_space=pltpu.SMEM), _rows(tm, D),
                  pl.BlockSpec(memory_space=pl.ANY)],
        out_specs=pl.BlockSpec(memory_space=pl.ANY),
        out_shape=jax.ShapeDtypeStruct(hs.shape, hs.dtype),
        scratch_shapes=[pltpu.SemaphoreType.DMA(())],
        input_output_aliases={2: 0},
        compiler_params=_cparams("arbitrary"),
        name="moe_dispatch",
    )(dest, x, hs)


def _moe_kernel(te_ref, nu_ref, x_ref, g_ref, *refs, nchunks):
    w_refs, (o_ref, h_scr, acc) = refs[:6], refs[6:]
    i = pl.program_id(0)
    j = pl.program_id(1)

    @pl.when(j == 0)
    def _():
        h_scr[...] = _rms(x_ref[...], g_ref[...]).astype(BF16)
        acc[...] = jnp.zeros_like(acc)

    @pl.when(i < nu_ref[0])
    def _():
        _swiglu_step(j, nchunks, h_scr, w_refs, acc)

    @pl.when(j == pl.num_programs(1) - 1)
    def _():
        o_ref[...] = acc[...]


def _moe_ffn(hs, g, tile_expert, n_used, w_gu, w_down, layer, sub, *, tm):
    Tp, D = hs.shape
    dff = w_down.shape[2]
    nchunks = dff // FF_CHUNK
    nsteps = (nchunks + 1) // 2
    step = lambda i, j, te, nu: jnp.where(i < nu[0], j, nsteps - 1)
    grid_spec = pltpu.PrefetchScalarGridSpec(
        num_scalar_prefetch=2,
        grid=(Tp // tm, nsteps),
        in_specs=[pl.BlockSpec((tm, D), lambda i, j, te, nu: (i, 0)), _layer(g, layer)]
        + _swiglu_specs(nchunks, D, lambda i, j, te, nu: (sub, te[i]), (None, None), step),
        out_specs=pl.BlockSpec((tm, D), lambda i, j, te, nu: (i, 0)),
        scratch_shapes=[pltpu.VMEM((tm, D), BF16), pltpu.VMEM((tm, D), F32)],
    )
    return pl.pallas_call(
        functools.partial(_moe_kernel, nchunks=nchunks),
        grid_spec=grid_spec,
        out_shape=jax.ShapeDtypeStruct((Tp, D), F32),
        compiler_params=_cparams("arbitrary", "arbitrary"),
        name="moe_ffn",
    )(tile_expert, n_used, hs, g, w_gu, w_gu, w_down, w_gu, w_gu, w_down)


def _combine_kernel(dest_ref, x_ref, wt_ref, y_ref, o_ref, buf1, buf2, sem):
    tm = x_ref.shape[0]

    def issue(r, _):
        _row_copy(y_ref, dest_ref[2 * r], buf1, r, sem).start()
        _row_copy(y_ref, dest_ref[2 * r + 1], buf2, r, sem).start()
        return 0

    def drain(r, _):
        for c in range(2):
            _row_copy(y_ref, 0, buf1, 0, sem).wait()
        return 0

    lax.fori_loop(0, tm, issue, 0, unroll=8)
    lax.fori_loop(0, tm, drain, 0, unroll=8)
    wt = wt_ref[...]
    o_ref[...] = x_ref[...] + (wt[:, 0:1] * buf1[...] + wt[:, 1:2] * buf2[...])


def _combine(x, wts, dest, y):
    T, D = x.shape
    tm = _tile(T, 512)
    return pl.pallas_call(
        _combine_kernel,
        grid=(T // tm,),
        in_specs=[pl.BlockSpec((2 * tm,), lambda i: (i,), memory_space=pltpu.SMEM), _rows(tm, D), _rows(tm, LANES),
                  pl.BlockSpec(memory_space=pl.ANY)],
        out_specs=_rows(tm, D),
        out_shape=jax.ShapeDtypeStruct((T, D), F32),
        scratch_shapes=[pltpu.VMEM((tm, D), F32), pltpu.VMEM((tm, D), F32), pltpu.SemaphoreType.DMA(())],
        compiler_params=_cparams("arbitrary"),
        name="moe_combine",
    )(dest, x, wts, y)


def _moe(xs, g, wr, w_gu, w_down, layer, sub):
    sizes = [x.shape[0] for x in xs]
    T = sum(sizes)
    D = xs[0].shape[1]
    tm = _tile(T, 1024)
    cnt = jnp.zeros((SUBLANES, LANES), F32)
    routed = []
    for x in xs:
        idx, wts, cnt = _router(x, g, wr, cnt, layer)
        routed.append((idx, wts))

    counts = cnt[0, :N_EXPERTS].astype(jnp.int32)
    padded = ((counts + tm - 1) // tm) * tm
    ends = jnp.cumsum(padded)
    starts = ends - padded
    n_tiles = (2 * T) // tm + N_EXPERTS
    Tp = n_tiles * tm
    tile_start = jnp.arange(n_tiles, dtype=jnp.int32) * tm
    tile_expert = jnp.sum((tile_start[:, None] >= ends[None, :]).astype(jnp.int32), axis=1)
    n_used = (ends[-1] // tm).astype(jnp.int32).reshape(1)
    tile_idx = jnp.arange(n_tiles, dtype=jnp.int32)
    last_expert = jnp.sum(jnp.where(tile_idx == n_used - 1, tile_expert, 0))
    tile_expert = jnp.where(tile_idx < n_used, tile_expert, last_expert).astype(jnp.int32)
    experts = jnp.arange(N_EXPERTS, dtype=jnp.int32)

    dests = []
    hs = jnp.zeros((Tp, D), F32)
    for x, (idx, _) in zip(xs, routed):
        e = idx[:, 0:2]
        dest = jnp.sum(jnp.where(e[:, :, None] == experts, starts, 0), axis=2) + idx[:, 2:4]
        dest = dest.reshape(-1)
        dests.append(dest)
        hs = _dispatch(x, dest, hs)
    y = _moe_ffn(hs, g, tile_expert, n_used, w_gu, w_down, layer, sub, tm=tm)
    return [_combine(x, wts, dest, y) for x, (_, wts), dest in zip(xs, routed, dests)]


def _norm_kernel(x_ref, g_ref, o_ref):
    o_ref[...] = _rms(x_ref[...], g_ref[...])


def _norm(x, g):
    T, D = x.shape
    tm = _tile(T, 1024)
    return pl.pallas_call(
        _norm_kernel,
        grid=(T // tm,),
        in_specs=[_rows(tm, D), _full(g.shape)],
        out_specs=_rows(tm, D),
        out_shape=jax.ShapeDtypeStruct((T, D), F32),
        compiler_params=_cparams("arbitrary"),
        name="final_norm",
    )(x, g)


def _s5_block_weights(coef_re, coef_im, b_re, b_im, c_re, c_im):
    depth, G, P, C = b_re.shape
    gpb = S5_CH // C
    nblk = G // gpb
    sst = gpb * P
    bbar_re = coef_re[..., None] * b_re - coef_im[..., None] * b_im
    bbar_im = coef_re[..., None] * b_im + coef_im[..., None] * b_re
    eye = jnp.eye(gpb, dtype=F32)

    def bblk(t):
        t = t.reshape(depth, nblk, gpb, P, C)
        return jnp.einsum('dnkpc,kj->dnkcjp', t, eye).reshape(depth, nblk, S5_CH, sst)

    def cblk(t):
        t = t.reshape(depth, nblk, gpb, C, P)
        return jnp.einsum('dnkcp,kj->dnkpjc', t, eye).reshape(depth, nblk, sst, S5_CH)

    bq = jnp.concatenate([bblk(bbar_re), bblk(bbar_im)], axis=3).astype(BF16)
    cq = jnp.concatenate([cblk(c_re), cblk(-c_im)], axis=2).astype(BF16)
    return bq, cq


def kernel(x_prompt, x_sample, mem_prompt, cache_mem_k, cache_mem_v, state_s5_re, state_s5_im, state_gla,
           norm_mix, w_in, s5_lambda_re, s5_lambda_im, s5_b_re, s5_b_im, s5_c_re, s5_c_im, s5_d, s5_log_dt,
           s5_w_glu, s5_b_glu, s5_norm, gla_w_gate2, gla_b_gate, gla_norm, w_out,
           norm_xa, norm_mem, xa_wq, xa_wk, xa_wv, xa_wo,
           norm_ffn, ffn_w_gu, ffn_w_down, moe_router, moe_w_gu, moe_w_down, norm_final):
    Bp, Lp, D = x_prompt.shape
    Bs, Ls, _ = x_sample.shape
    depth = w_in.shape[0]
    M = mem_prompt.shape[1]
    G, P = s5_lambda_re.shape[1:]
    ns = G * P
    s5w = G * S5_GROUP
    kw = GLA_HEADS * GLA_DK
    vw = GLA_HEADS * GLA_DV

    xp = x_prompt.reshape(Bp * Lp, D)
    xs = x_sample.reshape(Bs * Ls, D)
    mem = mem_prompt.reshape(Bp * M, D)
    vec = lambda v: v.reshape(depth, 1, -1)
    norm_mix, norm_xa, norm_mem, norm_ffn = vec(norm_mix), vec(norm_xa), vec(norm_mem), vec(norm_ffn)
    s5_b_glu, s5_norm, gla_b_gate, gla_norm, s5_d = (vec(s5_b_glu), vec(s5_norm), vec(gla_b_gate), vec(gla_norm),
                                                     vec(s5_d))
    zeros_s5 = jnp.zeros((Bp, ns), F32)
    zeros_gla = jnp.zeros((Bp, GLA_HEADS, GLA_DK, GLA_DV), F32)

    ar, ai, cfr, cfi = _s5_disc(s5_lambda_re.reshape(depth * G, P), s5_lambda_im.reshape(depth * G, P),
                                s5_log_dt.reshape(depth * G, 1))
    bq, cq = _s5_block_weights(cfr.reshape(depth, G, P), cfi.reshape(depth, G, P), s5_b_re, s5_b_im, s5_c_re, s5_c_im)
    ar = ar.reshape(depth, 1, ns)
    ai = ai.reshape(depth, 1, ns)

    outs = {k: [] for k in ('mk', 'mv', 'pr', 'pi', 'pg', 'sr', 'si', 'sg')}
    for i in range(depth):
        mk, mv = _norm_mm(mem, norm_mem, [xa_wk, xa_wv], i)
        mk = mk.reshape(Bp, M, D)
        mv = mv.reshape(Bp, M, D)
        outs['mk'].append(mk.reshape(Bp, M, XA_HEADS, D // XA_HEADS))
        outs['mv'].append(mv.reshape(Bp, M, XA_HEADS, D // XA_HEADS))

        new_x = []
        for grp, x, B, L in (('p', xp, Bp, Lp), ('s', xs, Bs, Ls)):
            prompt = grp == 'p'
            u, qkv, r, la = _mix_in(x.reshape(B, L, D), norm_mix, w_in, gla_w_gate2, gla_b_gate, i,
                                    su=s5w, sqkv=2 * kw + vw, sr=vw, tmajor=prompt)
            if prompt:
                u_tm = u
                h0r, h0i, s0 = zeros_s5, zeros_s5, zeros_gla
            else:
                u_tm = u.reshape(B, L, s5w // LANES, LANES).transpose(2, 1, 0, 3).reshape(s5w // LANES, L * B, LANES)
                h0r = state_s5_re[i].reshape(B, ns)
                h0i = state_s5_im[i].reshape(B, ns)
                s0 = state_gla
            y5, hr, hi = _s5(u_tm, h0r, h0i, ar, ai, bq, cq, s5_d, s5_w_glu, s5_b_glu, s5_norm, i, L=L, B=B)
            if not prompt:
                y5 = y5.reshape(s5w // LANES, L, B, LANES).transpose(2, 1, 0, 3).reshape(B, L, s5w)
            yg, sg = _gla(qkv, r, la, s0, gla_norm, i)
            outs[grp + 'r'].append(hr.reshape(B, G, P))
            outs[grp + 'i'].append(hi.reshape(B, G, P))
            outs[grp + 'g'].append(sg)
            x, q = _mix_out(x.reshape(B, L, D), y5, yg, w_out, norm_xa, xa_wq, i, tmajor=prompt)
            x = x.reshape(B * L, D)
            q = q.reshape(B * L, D)
            if prompt:
                o = _xattn(q, mk, mv, B=B, L=L)
            else:
                o = _xattn_cache(q, cache_mem_k, cache_mem_v, i, B=B, L=L)
            x = _mm_res(x, o, xa_wo, i)
            new_x.append(x)
        xp, xs = new_x
        if i % 2 == 0:
            xp = _ffn(xp, norm_ffn, ffn_w_gu, ffn_w_down, i, i // 2)
            xs = _ffn(xs, norm_ffn, ffn_w_gu, ffn_w_down, i, i // 2)
        else:
            wr = jnp.pad(moe_router[i // 2], ((0, 0), (0, LANES - N_EXPERTS)))
            xp, xs = _moe([xp, xs], norm_ffn, wr, moe_w_gu, moe_w_down, i, i // 2)

    y_prompt = _norm(xp, norm_final.reshape(1, D)).reshape(Bp, Lp, D)
    y_sample = _norm(xs, norm_final.reshape(1, D)).reshape(Bs, Ls, D)
    st = jnp.stack
    return (y_prompt, y_sample, st(outs['mk']), st(outs['mv']), st(outs['pr']), st(outs['pi']), st(outs['pg']),
            st(outs['sr']), st(outs['si']), st(outs['sg']))
```
